```python
import math
import jax, jax.numpy as jnp
from jax import lax
import numpy as np

D_MODEL = 1024
BATCH = 32
SEQ = 2048
DEPTH = 1

MEM_LEN = 256
EPS = 1e-6
D_FF = 2816
FFN_RES_WEIGHT = 0.5
CONV_A_WIDTH = D_MODEL
CONV_A_K = 3
SSM_D_INNER = 2 * D_MODEL
SSM_HEAD_DIM = 64
SSM_HEADS = SSM_D_INNER // SSM_HEAD_DIM
SSM_GROUPS = 4
SSM_STATE = 128
SSM_CONV_K = 4
SSM_CHUNK = 128
SSM_CONV_CH = SSM_D_INNER + 2 * SSM_GROUPS * SSM_STATE
XATTN_HEADS = 4
XATTN_HEAD_DIM = D_MODEL // XATTN_HEADS
XATTN_SCALE = 1.0 / math.sqrt(XATTN_HEAD_DIM)
IN_SIZES = (CONV_A_WIDTH, CONV_A_WIDTH, CONV_A_WIDTH,
            SSM_D_INNER, SSM_CONV_CH, SSM_HEADS,
            D_MODEL, D_MODEL)
D_IN_PROJ = sum(IN_SIZES)
IN_SPLITS = tuple(int(v) for v in np.cumsum(IN_SIZES)[:-1])

kernel_name = "hybrid_shortconv_ssd_gated_macaron"


def rmsnorm(x, g):
    xf = x.astype(jnp.float32)
    y = xf * lax.rsqrt(jnp.mean(xf * xf, axis=-1, keepdims=True) + EPS)
    return (y * g.astype(jnp.float32)).astype(x.dtype)


def swiglu(u, w_gate_up, w_down):
    gate, up = jnp.split(u @ w_gate_up, 2, axis=-1)
    return (jax.nn.silu(gate) * up) @ w_down


def causal_dwconv(x, w):
    k, c = w.shape
    return lax.conv_general_dilated(
        x, w[:, None, :].astype(x.dtype), window_strides=(1,),
        padding=[(k - 1, 0)], dimension_numbers=('NWC', 'WIO', 'NWC'),
        feature_group_count=c)


def short_conv_branch(b_gate, c_gate, v, conv_w, w_out):
    return (b_gate * causal_dwconv(c_gate * v, conv_w)) @ w_out


def ssd_chunked(xh, dt, a, bm, cm):
    b, s, h, p = xh.shape
    g, n = bm.shape[-2:]
    k = h // g
    l = SSM_CHUNK
    c = s // l
    x = (xh.astype(jnp.float32) * dt[..., None]).reshape(b, c, l, g, k, p)
    la = (dt * a).reshape(b, c, l, g, k).transpose(0, 1, 3, 4, 2)
    acs = jnp.cumsum(la, axis=-1)
    bc = bm.astype(jnp.float32).reshape(b, c, l, g, n)
    cc = cm.astype(jnp.float32).reshape(b, c, l, g, n)
    seg = acs[..., :, None] - acs[..., None, :]
    causal = jnp.tril(jnp.ones((l, l), dtype=bool))
    decay = jnp.exp(jnp.where(causal, seg, -jnp.inf))
    cb = jnp.einsum('bclgn,bcsgn->bcgls', cc, bc)
    y_diag = jnp.einsum('bcgls,bcgkls,bcsgkp->bclgkp', cb, decay, x)
    decay_to_end = jnp.exp(acs[..., -1:] - acs)
    states = jnp.einsum('bclgn,bcgkl,bclgkp->bcgkpn', bc, decay_to_end, x)
    chunk_decay = jnp.exp(acs[..., -1])

    def step(carry, inp):
        st, dec = inp
        return carry * dec[..., None, None] + st, carry

    init = jnp.zeros((b, g, k, p, n), jnp.float32)
    _, prev = lax.scan(step, init, (jnp.moveaxis(states, 1, 0), jnp.moveaxis(chunk_decay, 1, 0)))
    prev = jnp.moveaxis(prev, 0, 1)
    y_off = jnp.einsum('bclgn,bcgkpn,bcgkl->bclgkp', cc, prev, jnp.exp(acs))
    return (y_diag + y_off).reshape(b, s, h, p)


def mamba2_branch(z, xbc, dt_raw, conv_w, conv_b, dt_bias, a_log, d_skip, norm_g, w_out):
    xbc = jax.nn.silu(causal_dwconv(xbc, conv_w) + conv_b.astype(xbc.dtype))
    xs, bm, cm = jnp.split(xbc, (SSM_D_INNER, SSM_D_INNER + SSM_GROUPS * SSM_STATE), axis=-1)
    b, s, _ = xs.shape
    xh = xs.reshape(b, s, SSM_HEADS, SSM_HEAD_DIM)
    dt = jax.nn.softplus(dt_raw.astype(jnp.float32) + dt_bias.astype(jnp.float32))
    a = -jnp.exp(a_log.astype(jnp.float32))
    y = ssd_chunked(xh, dt, a,
                    bm.reshape(b, s, SSM_GROUPS, SSM_STATE),
                    cm.reshape(b, s, SSM_GROUPS, SSM_STATE))
    y = y + d_skip.astype(jnp.float32)[:, None] * xh.astype(jnp.float32)
    yg = (y.reshape(b, s, SSM_D_INNER) * jax.nn.silu(z.astype(jnp.float32)))
    yg = yg.reshape(b, s, SSM_GROUPS, SSM_D_INNER // SSM_GROUPS)
    yg = yg * lax.rsqrt(jnp.mean(yg * yg, axis=-1, keepdims=True) + EPS)
    y = (yg.reshape(b, s, SSM_D_INNER) * norm_g.astype(jnp.float32)).astype(z.dtype)
    return y @ w_out


def memory_cross_attention(u, mem_n, w_q, w_kv, w_o):
    b, s, _ = u.shape
    m = mem_n.shape[1]
    q = (u @ w_q).reshape(b, s, XATTN_HEADS, XATTN_HEAD_DIM)
    k, v = jnp.split(mem_n @ w_kv, 2, axis=-1)
    k = k.reshape(b, m, XATTN_HEADS, XATTN_HEAD_DIM)
    v = v.reshape(b, m, XATTN_HEADS, XATTN_HEAD_DIM)
    scores = jnp.einsum('bshd,bmhd->bhsm', q, k).astype(jnp.float32) * XATTN_SCALE
    probs = jax.nn.softmax(scores, axis=-1).astype(v.dtype)
    o = jnp.einsum('bhsm,bmhd->bshd', probs, v).reshape(b, s, D_MODEL)
    return o @ w_o


def setup_inputs(seed: int = 0) -> dict:
    key = jax.random.key(seed)
    ks = iter(jax.random.split(key, 40))

    def w(shape, fan_in):
        return jax.random.normal(next(ks), shape, jnp.float32) * (fan_in ** -0.5)

    def gain(shape):
        return 1.0 + 0.02 * jax.random.normal(next(ks), shape, jnp.float32)

    L = DEPTH
    x = jax.random.normal(next(ks), (BATCH, SEQ, D_MODEL), jnp.float32)
    mem = jax.random.normal(next(ks), (BATCH, MEM_LEN, D_MODEL), jnp.float32)
    dt0 = jnp.exp(jax.random.uniform(next(ks), (L, SSM_HEADS), jnp.float32,
                                     math.log(1e-3), math.log(1e-1)))
    dt_bias = dt0 + jnp.log(-jnp.expm1(-dt0))
    a_log = jnp.log(jax.random.uniform(next(ks), (L, SSM_HEADS), jnp.float32, 1.0, 16.0))
    return {
        "x": x,
        "mem": mem,
        "ffn1_norm": gain((L, D_MODEL)),
        "ffn1_w_gate_up": w((L, D_MODEL, 2 * D_FF), D_MODEL),
        "ffn1_w_down": w((L, D_FF, D_MODEL), D_FF),
        "mix_norm": gain((L, D_MODEL)),
        "w_in": w((L, D_MODEL, D_IN_PROJ), D_MODEL),
        "conv_a_w": w((L, CONV_A_K, CONV_A_WIDTH), CONV_A_K),
        "w_out_a": w((L, CONV_A_WIDTH, D_MODEL), CONV_A_WIDTH),
        "ssm_conv_w": w((L, SSM_CONV_K, SSM_CONV_CH), SSM_CONV_K),
        "ssm_conv_b": 0.02 * jax.random.normal(next(ks), (L, SSM_CONV_CH), jnp.float32),
        "ssm_dt_bias": dt_bias,
        "ssm_a_log": a_log,
        "ssm_d": gain((L, SSM_HEADS)),
        "ssm_norm": gain((L, SSM_D_INNER)),
        "w_out_ssm": w((L, SSM_D_INNER, D_MODEL), SSM_D_INNER),
        "w_mix_out": w((L, D_MODEL, D_MODEL), D_MODEL),
        "xattn_norm": gain((L, D_MODEL)),
        "mem_norm": gain((L, D_MODEL)),
        "w_q": w((L, D_MODEL, D_MODEL), D_MODEL),
        "w_kv": w((L, D_MODEL, 2 * D_MODEL), D_MODEL),
        "w_o_x": w((L, D_MODEL, D_MODEL), D_MODEL),
        "ffn2_norm": gain((L, D_MODEL)),
        "ffn2_w_gate_up": w((L, D_MODEL, 2 * D_FF), D_MODEL),
        "ffn2_w_down": w((L, D_FF, D_MODEL), D_FF),
        "final_norm": gain((D_MODEL,)),
    }


def reference(x, mem, ffn1_norm, ffn1_w_gate_up, ffn1_w_down, mix_norm, w_in, conv_a_w,
              w_out_a, ssm_conv_w, ssm_conv_b, ssm_dt_bias, ssm_a_log, ssm_d, ssm_norm,
              w_out_ssm, w_mix_out, xattn_norm, mem_norm, w_q, w_kv, w_o_x,
              ffn2_norm, ffn2_w_gate_up, ffn2_w_down, final_norm):
    h = x
    for i in range(DEPTH):
        h = h + FFN_RES_WEIGHT * swiglu(rmsnorm(h, ffn1_norm[i]), ffn1_w_gate_up[i], ffn1_w_down[i])
        u = rmsnorm(h, mix_norm[i])
        proj = u @ w_in[i]
        a_b, a_c, a_v, z, xbc, dt_raw, g_a, g_b = jnp.split(proj, IN_SPLITS, axis=-1)
        y_a = short_conv_branch(a_b, a_c, a_v, conv_a_w[i], w_out_a[i])
        y_b = mamba2_branch(z, xbc, dt_raw, ssm_conv_w[i], ssm_conv_b[i], ssm_dt_bias[i],
                            ssm_a_log[i], ssm_d[i], ssm_norm[i], w_out_ssm[i])
        merged = jax.nn.sigmoid(g_a) * y_a + jax.nn.sigmoid(g_b) * y_b
        h = h + merged @ w_mix_out[i]
        h = h + memory_cross_attention(rmsnorm(h, xattn_norm[i]), rmsnorm(mem, mem_norm[i]),
                                       w_q[i], w_kv[i], w_o_x[i])
        h = h + FFN_RES_WEIGHT * swiglu(rmsnorm(h, ffn2_norm[i]), ffn2_w_gate_up[i], ffn2_w_down[i])
    return rmsnorm(h, final_norm)
```

```python
import functools
import math

import jax
import jax.numpy as jnp
from jax import lax
from jax.experimental import pallas as pl
from jax.experimental.pallas import tpu as pltpu

F32 = jnp.float32
BF16 = jnp.bfloat16

EPS = 1e-6
FFN_RES_WEIGHT = 0.5
SSM_HEAD_DIM = 64
SSM_GROUPS = 4
SSM_STATE = 128
SSM_CHUNK = 128
XATTN_HEADS = 4

LANES = 128
SUBLANES = 8
MXU_DIM = 256
VMEM_LIMIT_BYTES = 60 * 1024 * 1024

FFN_TOKENS = 512
PROJ_TOKENS = 512
MIX_TOKENS = 256
ATTN_TOKENS = 512
KV_TOKENS = 256


def _dot(a, b):
    return jnp.dot(a, b, preferred_element_type=F32)


def _dot_nt(a, b):
    return lax.dot_general(a, b, (((1,), (1,)), ((), ())), preferred_element_type=F32)


def _rms(x, g):
    ms = jnp.mean(x * x, axis=-1, keepdims=True)
    return x * lax.rsqrt(ms + EPS) * g


def _silu(x):
    return x * jax.nn.sigmoid(x)


def _resident(shape):
    zeros = (0,) * len(shape)
    return pl.BlockSpec(shape, lambda *_: zeros, pipeline_mode=pl.Buffered(1))


def _params(*semantics):
    return pltpu.CompilerParams(dimension_semantics=semantics,
                                vmem_limit_bytes=VMEM_LIMIT_BYTES)


def _ff_chunks(d_ff):
    chunks, start = [], 0
    while start < d_ff:
        n = min(4 * MXU_DIM, d_ff - start)
        chunks.append((start, n))
        start += n
    return tuple(chunks)


def _ffn_body(x_ref, g_ref, wg_ref, wu_ref, wd_ref, *rest, chunks, final):
    if final:
        fg_ref, o_ref = rest
    else:
        (o_ref,) = rest
    x = x_ref[...]
    u = _rms(x, g_ref[...]).astype(BF16)
    acc = None
    for start, n in chunks:
        gate = _dot(u, wg_ref[:, start:start + n])
        up = _dot(u, wu_ref[:, start:start + n])
        act = (_silu(gate) * up).astype(BF16)
        part = _dot(act, wd_ref[start:start + n, :])
        acc = part if acc is None else acc + part
    h = x + FFN_RES_WEIGHT * acc
    if final:
        h = _rms(h, fg_ref[...])
    o_ref[...] = h


def _ffn(x, norm_g, w_gate_up, w_down, final_g=None):
    n_tok, d = x.shape
    d_ff = w_down.shape[0]
    wg = w_gate_up[:, :d_ff].astype(BF16)
    wu = w_gate_up[:, d_ff:].astype(BF16)
    wd = w_down.astype(BF16)
    tm = FFN_TOKENS
    tile = pl.BlockSpec((tm, d), lambda i: (i, 0))
    in_specs = [tile, _resident((1, d)), _resident((d, d_ff)), _resident((d, d_ff)),
                _resident((d_ff, d))]
    args = [x, norm_g.reshape(1, d), wg, wu, wd]
    if final_g is not None:
        in_specs.append(_resident((1, d)))
        args.append(final_g.reshape(1, d))
    return pl.pallas_call(
        functools.partial(_ffn_body, chunks=_ff_chunks(d_ff), final=final_g is not None),
        grid=(n_tok // tm,),
        in_specs=in_specs,
        out_specs=tile,
        out_shape=jax.ShapeDtypeStruct((n_tok, d), F32),
        compiler_params=_params("parallel"),
        name="ffn_final" if final_g is not None else "ffn",
    )(*args)


def _inproj_body(h_ref, g_ref, wm_ref, wdt_ref, *out_refs, widths):
    u = _rms(h_ref[...], g_ref[...]).astype(BF16)
    off = 0
    for o_ref, n in zip(out_refs[:-1], widths):
        for s in range(0, n, 4 * MXU_DIM):
            w = min(4 * MXU_DIM, n - s)
            o_ref[:, s:s + w] = _dot(u, wm_ref[:, off + s:off + s + w]).astype(BF16)
        off += n
    out_refs[-1][...] = _dot(u, wdt_ref[...])


def _inproj(h, norm_g, w_main, w_dt, widths):
    n_tok, d = h.shape
    tm = PROJ_TOKENS
    row = lambda i: (i, 0)
    out_shape = [jax.ShapeDtypeStruct((n_tok, n), BF16) for n in widths]
    out_shape.append(jax.ShapeDtypeStruct((n_tok, LANES), F32))
    out_specs = [pl.BlockSpec((tm, n), row) for n in widths]
    out_specs.append(pl.BlockSpec((tm, LANES), row))
    return pl.pallas_call(
        functools.partial(_inproj_body, widths=widths),
        grid=(n_tok // tm,),
        in_specs=[pl.BlockSpec((tm, d), row), _resident((1, d)),
                  _resident(w_main.shape), _resident(w_dt.shape)],
        out_specs=out_specs,
        out_shape=out_shape,
        compiler_params=_params("parallel"),
        name="inproj",
    )(h, norm_g.reshape(1, d), w_main, w_dt)


CONV_ROWS = 16
CONV_LANES = 512
POST_ROWS = 32


def _causal_conv(x_ref, halo_ref, taps_ref, emit, *, lanes, rows, prep=None):
    n_rows = x_ref.shape[0]
    taps = taps_ref[:, lanes]
    k_taps = taps.shape[0]

    def load(r):
        return x_ref[pl.ds(r, rows), lanes].astype(F32) if prep is None else prep(r)

    def step(i, carry):
        r = pl.multiple_of(i * rows, rows)
        x = load(r)
        both = jnp.concatenate([carry, x], axis=0)
        acc = x * taps[k_taps - 1:k_taps, :]
        for back in range(1, k_taps):
            shifted = pltpu.roll(both, back, axis=0)[SUBLANES:, :]
            acc = acc + shifted * taps[k_taps - 1 - back:k_taps - back, :]
        emit(r, x, acc)
        return x[rows - SUBLANES:, :]

    halo_ref[:, lanes] = lax.fori_loop(0, n_rows // rows, step, halo_ref[:, lanes])


def _split3(v):
    hi = v.astype(BF16)
    r1 = v - hi.astype(F32)
    mid = r1.astype(BF16)
    lo = (r1 - mid.astype(F32)).astype(BF16)
    return hi, mid, lo


def _lane_bcast(m, col, width):
    return jnp.broadcast_to(m[:, col:col + 1], (m.shape[0], width))


def _mixer_body(ab_ref, ac_ref, av_ref, z_ref, xbc_ref, ga_ref, gb_ref, dt_ref, h_ref,
                caw_ref, csw_ref, csb_ref, dtb_ref, alog_ref, dch_ref, ng_ref,
                woa_ref, wos_ref, wmx_ref, o_ref,
                state_ref, halo_x_ref, halo_c_ref, xs_ref, ya_ref, la_ref, dts_ref,
                y_ref, yn_ref, *, d_inner, n_groups, n_state):
    t_tile = h_ref.shape[0]
    chunk = SSM_CHUNK
    gw = d_inner // n_groups
    pairs = gw // LANES
    hpg = gw // SSM_HEAD_DIM
    b_off, c_off = d_inner, d_inner + n_groups * n_state

    @pl.when(pl.program_id(1) == 0)
    def _():
        state_ref[...] = jnp.zeros_like(state_ref)
        halo_x_ref[...] = jnp.zeros_like(halo_x_ref)
        halo_c_ref[...] = jnp.zeros_like(halo_c_ref)

    for s in range(0, ac_ref.shape[1], CONV_LANES):
        lanes = slice(s, s + CONV_LANES)

        def prep_cv(r, lanes=lanes):
            return (ac_ref[pl.ds(r, CONV_ROWS), lanes].astype(F32)
                    * av_ref[pl.ds(r, CONV_ROWS), lanes].astype(F32))

        def emit_a(r, x, conv, lanes=lanes):
            b = ab_ref[pl.ds(r, CONV_ROWS), lanes].astype(F32)
            ya_ref[pl.ds(r, CONV_ROWS), lanes] = (b * conv).astype(BF16)

        _causal_conv(ac_ref, halo_c_ref, caw_ref, emit_a, lanes=lanes, rows=CONV_ROWS,
                     prep=prep_cv)

    for s in range(0, xbc_ref.shape[1], CONV_LANES):
        lanes = slice(s, s + CONV_LANES)

        def emit_x(r, x, conv, lanes=lanes):
            xs_ref[pl.ds(r, CONV_ROWS), lanes] = _silu(conv + csb_ref[:, lanes])

        _causal_conv(xbc_ref, halo_x_ref, csw_ref, emit_x, lanes=lanes, rows=CONV_ROWS)

    dt_all = jax.nn.softplus(dt_ref[...] + dtb_ref[...])
    dts_ref[...] = dt_all
    la_ref[...] = dt_all * (-jnp.exp(alog_ref[...]))

    row_i = lax.broadcasted_iota(jnp.int32, (chunk, chunk), 0)
    col_i = lax.broadcasted_iota(jnp.int32, (chunk, chunk), 1)
    causal = row_i >= col_i
    tril = causal.astype(BF16)
    low_half = col_i < SSM_HEAD_DIM

    def chunk_step(c, _):
        r0 = pl.multiple_of(c * chunk, chunk)
        rows = pl.ds(r0, chunk)
        hi, mid, lo = _split3(la_ref[rows, :])
        acs = _dot(tril, hi) + _dot(tril, mid) + _dot(tril, lo)
        acs_t = acs.T
        dtc = dts_ref[rows, :]
        for g in range(n_groups):
            bm = xs_ref[rows, b_off + g * n_state:b_off + (g + 1) * n_state]
            cm = xs_ref[rows, c_off + g * n_state:c_off + (g + 1) * n_state].astype(BF16)
            cb = _dot_nt(cm, bm.astype(BF16))
            state = state_ref[g]
            y_off = _dot(cm, state.astype(BF16))
            xstate, decay_end = [], []
            for q in range(pairs):
                h0 = g * hpg + 2 * q
                ch = slice(g * gw + q * LANES, g * gw + (q + 1) * LANES)
                a0 = _lane_bcast(acs, h0, chunk)
                a1 = _lane_bcast(acs, h0 + 1, chunk)
                m0 = (cb * jnp.where(causal, jnp.exp(a0 - acs_t[h0:h0 + 1, :]), 0.0))
                m1 = (cb * jnp.where(causal, jnp.exp(a1 - acs_t[h0 + 1:h0 + 2, :]), 0.0))
                a_pair = jnp.where(low_half, a0, a1)
                dt_pair = jnp.where(low_half, _lane_bcast(dtc, h0, LANES),
                                    _lane_bcast(dtc, h0 + 1, LANES))
                xp = xs_ref[rows, ch]
                xdt = xp * dt_pair
                y_diag = (_dot(m0.astype(BF16), jnp.where(low_half, xdt, 0.0).astype(BF16))
                          + _dot(m1.astype(BF16), jnp.where(low_half, 0.0, xdt).astype(BF16)))
                y_ref[rows, ch] = (y_diag + y_off[:, q * LANES:(q + 1) * LANES] * jnp.exp(a_pair)
                                   + dch_ref[:, ch] * xp)
                total = a_pair[chunk - 1:chunk, :]
                xstate.append((xdt * jnp.exp(total - a_pair)).astype(BF16))
                decay_end.append(jnp.exp(total))
            xstate = jnp.concatenate(xstate, axis=1)
            decay_end = jnp.concatenate(decay_end, axis=1)
            state_ref[g] = state * decay_end + _dot(bm.T.astype(BF16), xstate)
        return 0

    lax.fori_loop(0, t_tile // chunk, chunk_step, 0)

    def post_step(i, _):
        r = pl.ds(pl.multiple_of(i * POST_ROWS, POST_ROWS), POST_ROWS)
        for g in range(n_groups):
            ch = slice(g * gw, (g + 1) * gw)
            yg = y_ref[r, ch] * _silu(z_ref[r, ch].astype(F32))
            ms = jnp.mean(yg * yg, axis=-1, keepdims=True)
            yn_ref[r, ch] = (yg * lax.rsqrt(ms + EPS) * ng_ref[:, ch]).astype(BF16)
        return 0

    lax.fori_loop(0, t_tile // POST_ROWS, post_step, 0)

    y_a = _dot(ya_ref[...], woa_ref[...])
    y_b = _dot(yn_ref[...], wos_ref[...])
    merged = (jax.nn.sigmoid(ga_ref[...].astype(F32)) * y_a
              + jax.nn.sigmoid(gb_ref[...].astype(F32)) * y_b)
    o_ref[...] = h_ref[...] + _dot(merged.astype(BF16), wmx_ref[...])


def _mixer(h, proj, conv_a_w, ssm_conv_w, ssm_conv_b, dt_bias, a_log, d_skip, norm_g,
           w_out_a, w_out_ssm, w_mix_out, batch, seq):
    ab, ac, av, z, xbc, ga, gb, dt = proj
    n_tok, d = h.shape
    d_inner = z.shape[1]
    n_heads = d_inner // SSM_HEAD_DIM
    conv_ch = xbc.shape[1]
    t = MIX_TOKENS
    tiles = seq // t
    row = lambda b, j: (b * tiles + j, 0)

    def lane_pad(v):
        return jnp.pad(v.astype(F32), (0, LANES - n_heads)).reshape(1, LANES)

    vec_args = [conv_a_w.astype(F32), ssm_conv_w.astype(F32), ssm_conv_b.reshape(1, conv_ch),
                lane_pad(dt_bias), lane_pad(a_log),
                jnp.repeat(d_skip.astype(F32), SSM_HEAD_DIM).reshape(1, d_inner),
                norm_g.reshape(1, d_inner)]
    w_args = [w_out_a.astype(BF16), w_out_ssm.astype(BF16), w_mix_out.astype(BF16)]
    tok_args = [ab, ac, av, z, xbc, ga, gb, dt, h]
    in_specs = ([pl.BlockSpec((t, a.shape[1]), row) for a in tok_args]
                + [_resident(a.shape) for a in vec_args + w_args])
    n_state = SSM_STATE
    scratch = [
        pltpu.VMEM((SSM_GROUPS, n_state, d_inner // SSM_GROUPS), F32),
        pltpu.VMEM((SUBLANES, conv_ch), F32),
        pltpu.VMEM((SUBLANES, ac.shape[1]), F32),
        pltpu.VMEM((t, conv_ch), F32),
        pltpu.VMEM((t, ac.shape[1]), BF16),
        pltpu.VMEM((t, LANES), F32),
        pltpu.VMEM((t, LANES), F32),
        pltpu.VMEM((t, d_inner), F32),
        pltpu.VMEM((t, d_inner), BF16),
    ]
    return pl.pallas_call(
        functools.partial(_mixer_body, d_inner=d_inner, n_groups=SSM_GROUPS, n_state=n_state),
        grid=(batch, tiles),
        in_specs=in_specs,
        out_specs=pl.BlockSpec((t, d), row),
        out_shape=jax.ShapeDtypeStruct((n_tok, d), F32),
        scratch_shapes=scratch,
        compiler_params=_params("parallel", "arbitrary"),
        name="mixer",
    )(*tok_args, *vec_args, *w_args)


def _kv_body(m_ref, g_ref, w_ref, o_ref):
    u = _rms(m_ref[...], g_ref[...]).astype(BF16)
    o_ref[...] = _dot(u, w_ref[...]).astype(BF16)


def _kv_proj(mem2d, norm_g, w_kv):
    n_tok, d = mem2d.shape
    tm = KV_TOKENS
    return pl.pallas_call(
        _kv_body,
        grid=(n_tok // tm,),
        in_specs=[pl.BlockSpec((tm, d), lambda i: (i, 0)), _resident((1, d)),
                  _resident(w_kv.shape)],
        out_specs=pl.BlockSpec((tm, w_kv.shape[1]), lambda i: (i, 0)),
        out_shape=jax.ShapeDtypeStruct((n_tok, w_kv.shape[1]), BF16),
        compiler_params=_params("parallel"),
        name="kv_proj",
    )(mem2d, norm_g.reshape(1, d), w_kv.astype(BF16))


def _xattn_body(h_ref, g_ref, wq_ref, k_ref, v_ref, wo_ref, o_ref, *, heads):
    x = h_ref[...]
    d = x.shape[1]
    dh = d // heads
    scale = 1.0 / math.sqrt(dh)
    u = _rms(x, g_ref[...]).astype(BF16)
    q = _dot(u, wq_ref[...]).astype(BF16)
    outs = []
    for hd in range(heads):
        ch = slice(hd * dh, (hd + 1) * dh)
        s = _dot_nt(q[:, ch], k_ref[0, :, ch]) * scale
        p = jnp.exp(s - jnp.max(s, axis=-1, keepdims=True))
        p = p / jnp.sum(p, axis=-1, keepdims=True)
        outs.append(_dot(p.astype(BF16), v_ref[0, :, ch]).astype(BF16))
    o_ref[...] = x + _dot(jnp.concatenate(outs, axis=1), wo_ref[...])


def _xattn(h, kv, norm_g, w_q, w_o, batch, seq):
    n_tok, d = h.shape
    m = kv.shape[0] // batch
    kv3 = kv.reshape(batch, m, 2 * d)
    tm = ATTN_TOKENS
    tiles = seq // tm
    row = lambda b, j: (b * tiles + j, 0)
    return pl.pallas_call(
        functools.partial(_xattn_body, heads=XATTN_HEADS),
        grid=(batch, tiles),
        in_specs=[pl.BlockSpec((tm, d), row), _resident((1, d)), _resident((d, d)),
                  pl.BlockSpec((1, m, d), lambda b, j: (b, 0, 0)),
                  pl.BlockSpec((1, m, d), lambda b, j: (b, 0, 1)),
                  _resident((d, d))],
        out_specs=pl.BlockSpec((tm, d), row),
        out_shape=jax.ShapeDtypeStruct((n_tok, d), F32),
        compiler_params=_params("parallel", "parallel"),
        name="xattn",
    )(h, norm_g.reshape(1, d), w_q.astype(BF16), kv3, kv3, w_o.astype(BF16))


def _split_w_in(w_in, d, d_inner, conv_ch, n_heads):
    sizes = (d, d, d, d_inner, conv_ch, n_heads, d, d)
    offs = [0]
    for n in sizes:
        offs.append(offs[-1] + n)
    piece = lambda i: w_in[:, offs[i]:offs[i + 1]]
    main = jnp.concatenate([piece(i) for i in (0, 1, 2, 3, 4, 6, 7)], axis=1).astype(BF16)
    w_dt = jnp.pad(piece(5), ((0, 0), (0, LANES - n_heads))).astype(BF16)
    widths = tuple(sizes[i] for i in (0, 1, 2, 3, 4, 6, 7))
    return main, w_dt, widths


def kernel(x, mem, ffn1_norm, ffn1_w_gate_up, ffn1_w_down, mix_norm, w_in, conv_a_w, w_out_a, ssm_conv_w, ssm_conv_b, ssm_dt_bias, ssm_a_log, ssm_d, ssm_norm, w_out_ssm, w_mix_out, xattn_norm, mem_norm, w_q, w_kv, w_o_x, ffn2_norm, ffn2_w_gate_up, ffn2_w_down, final_norm):
    batch, seq, d = x.shape
    depth = w_in.shape[0]
    d_inner = w_out_ssm.shape[1]
    conv_ch = ssm_conv_w.shape[2]
    n_heads = ssm_d.shape[1]
    h = x.reshape(batch * seq, d)
    mem2d = mem.reshape(-1, d)
    for i in range(depth):
        last = i == depth - 1
        h = _ffn(h, ffn1_norm[i], ffn1_w_gate_up[i], ffn1_w_down[i])
        w_main, w_dt, widths = _split_w_in(w_in[i], d, d_inner, conv_ch, n_heads)
        proj = _inproj(h, mix_norm[i], w_main, w_dt, widths)
        h = _mixer(h, proj, conv_a_w[i], ssm_conv_w[i], ssm_conv_b[i], ssm_dt_bias[i],
                   ssm_a_log[i], ssm_d[i], ssm_norm[i], w_out_a[i], w_out_ssm[i],
                   w_mix_out[i], batch, seq)
        kv = _kv_proj(mem2d, mem_norm[i], w_kv[i])
        h = _xattn(h, kv, xattn_norm[i], w_q[i], w_o_x[i], batch, seq)
        h = _ffn(h, ffn2_norm[i], ffn2_w_gate_up[i], ffn2_w_down[i],
                 final_g=final_norm if last else None)
    return h.reshape(batch, seq, d)
```

```python
import functools
import math

import jax
import jax.numpy as jnp
from jax import lax
from jax.experimental import pallas as pl
from jax.experimental.pallas import tpu as pltpu

F32 = jnp.float32
BF16 = jnp.bfloat16

EPS = 1e-6
FFN_RES_WEIGHT = 0.5
SSM_HEAD_DIM = 64
SSM_GROUPS = 4
SSM_STATE = 128
SSM_CHUNK = 128
XATTN_HEADS = 4
LOG2E = math.log2(math.e)

LANES = 128
SUBLANES = 8
MXU_DIM = 256
VMEM_LIMIT_BYTES = 60 * 1024 * 1024

FFN_TOKENS = 512
PROJ_TOKENS = 512
PROJ_COLS = 1024
MIX_TOKENS = 256
ATTN_TOKENS = 512
KV_TOKENS = 256


def _dot(a, b):
    return jnp.dot(a, b, preferred_element_type=F32)


def _dot_nt(a, b):
    return lax.dot_general(a, b, (((1,), (1,)), ((), ())), preferred_element_type=F32)


def _dot_tn(a, b):
    return lax.dot_general(a, b, (((0,), (0,)), ((), ())), preferred_element_type=F32)


def _rms(x, g):
    ms = jnp.mean(x * x, axis=-1, keepdims=True)
    return x * lax.rsqrt(ms + EPS) * g


def _silu(x):
    return x * jax.nn.sigmoid(x)


def _resident(shape):
    zeros = (0,) * len(shape)
    return pl.BlockSpec(shape, lambda *_: zeros, pipeline_mode=pl.Buffered(1))


def _params(*semantics):
    return pltpu.CompilerParams(dimension_semantics=semantics,
                                vmem_limit_bytes=VMEM_LIMIT_BYTES)


def _ff_chunks(d_ff):
    chunks, start = [], 0
    while start < d_ff:
        n = min(4 * MXU_DIM, d_ff - start)
        chunks.append((start, n))
        start += n
    return tuple(chunks)


def _ffn_body(x_ref, g_ref, wg_ref, wu_ref, wd_ref, *rest, chunks, final):
    if final:
        fg_ref, o_ref = rest
    else:
        (o_ref,) = rest
    x = x_ref[...]
    u = _rms(x, g_ref[...]).astype(BF16)
    acc = None
    for start, n in chunks:
        gate = _dot(u, wg_ref[:, start:start + n])
        up = _dot(u, wu_ref[:, start:start + n])
        act = (_silu(gate) * up).astype(BF16)
        part = _dot(act, wd_ref[start:start + n, :])
        acc = part if acc is None else acc + part
    h = x + FFN_RES_WEIGHT * acc
    if final:
        h = _rms(h, fg_ref[...])
    o_ref[...] = h


def _ffn(x, norm_g, w_gate_up, w_down, final_g=None):
    n_tok, d = x.shape
    d_ff = w_down.shape[0]
    wg = w_gate_up[:, :d_ff].astype(BF16)
    wu = w_gate_up[:, d_ff:].astype(BF16)
    wd = w_down.astype(BF16)
    tm = FFN_TOKENS
    tile = pl.BlockSpec((tm, d), lambda i: (i, 0))
    in_specs = [tile, _resident((1, d)), _resident((d, d_ff)), _resident((d, d_ff)),
                _resident((d_ff, d))]
    args = [x, norm_g.reshape(1, d), wg, wu, wd]
    if final_g is not None:
        in_specs.append(_resident((1, d)))
        args.append(final_g.reshape(1, d))
    return pl.pallas_call(
        functools.partial(_ffn_body, chunks=_ff_chunks(d_ff), final=final_g is not None),
        grid=(n_tok // tm,),
        in_specs=in_specs,
        out_specs=tile,
        out_shape=jax.ShapeDtypeStruct((n_tok, d), F32),
        compiler_params=_params("parallel"),
        name="ffn_final" if final_g is not None else "ffn",
    )(*args)


def _conv_tile(x, pad_ref, halo_ref, cols, taps):
    tm = x.shape[0]
    k_taps = taps.shape[0]
    pad_ref[0:SUBLANES, :] = halo_ref[:, cols]
    pad_ref[SUBLANES:SUBLANES + tm, :] = x
    halo_ref[:, cols] = x[tm - SUBLANES:, :]
    acc = x * taps[k_taps - 1:k_taps, :]
    for back in range(1, k_taps):
        shifted = pad_ref[SUBLANES - back:SUBLANES - back + tm, :]
        acc = acc + shifted * taps[k_taps - 1 - back:k_taps - back, :]
    return acc


def _inproj_body(h_ref, g_ref, wm_ref, wdt_ref, caw_ref, csw_ref, csb_ref, dtb_ref,
                 ya_ref, sz_ref, xs_ref, ga_ref, gb_ref, dt_ref,
                 pad_ref, halo_c_ref, halo_x_ref, *, tiles_per_seq):
    d = h_ref.shape[1]
    d_inner = sz_ref.shape[1]
    conv_ch = xs_ref.shape[1]
    cw = PROJ_COLS

    @pl.when(pl.program_id(0) % tiles_per_seq == 0)
    def _():
        halo_c_ref[...] = jnp.zeros_like(halo_c_ref)
        halo_x_ref[...] = jnp.zeros_like(halo_x_ref)

    u = _rms(h_ref[...], g_ref[...]).astype(BF16)

    def proj(off):
        return _dot(u, wm_ref[:, off:off + cw])

    for s in range(0, d, cw):
        cols = slice(s, s + cw)
        conv = _conv_tile(proj(d + s) * proj(2 * d + s), pad_ref, halo_c_ref, cols,
                          caw_ref[:, cols])
        ya_ref[:, cols] = (proj(s) * conv).astype(BF16)
    off = 3 * d
    for s in range(0, d_inner, cw):
        sz_ref[:, s:s + cw] = _silu(proj(off + s)).astype(BF16)
    off += d_inner
    for s in range(0, conv_ch, cw):
        cols = slice(s, s + cw)
        conv = _conv_tile(proj(off + s), pad_ref, halo_x_ref, cols, csw_ref[:, cols])
        xs_ref[:, cols] = _silu(conv + csb_ref[:, cols]).astype(BF16)
    off += conv_ch
    for s in range(0, d, cw):
        ga_ref[:, s:s + cw] = jax.nn.sigmoid(proj(off + s)).astype(BF16)
        gb_ref[:, s:s + cw] = jax.nn.sigmoid(proj(off + d + s)).astype(BF16)
    dt_ref[...] = jax.nn.softplus(_dot(u, wdt_ref[...]) + dtb_ref[...])


def _inproj(h, norm_g, w_main, w_dt, conv_a_w, ssm_conv_w, ssm_conv_b, dt_bias_row,
            d_inner, seq):
    n_tok, d = h.shape
    conv_ch = ssm_conv_w.shape[1]
    tm = PROJ_TOKENS
    row = lambda i: (i, 0)
    widths = (d, d_inner, conv_ch, d, d)
    out_shape = [jax.ShapeDtypeStruct((n_tok, n), BF16) for n in widths]
    out_shape.append(jax.ShapeDtypeStruct((n_tok, LANES), F32))
    out_specs = [pl.BlockSpec((tm, n), row) for n in widths]
    out_specs.append(pl.BlockSpec((tm, LANES), row))
    vec_args = [norm_g.reshape(1, d), w_main, w_dt, conv_a_w.astype(F32),
                ssm_conv_w.astype(F32), ssm_conv_b.reshape(1, conv_ch).astype(F32),
                dt_bias_row]
    return pl.pallas_call(
        functools.partial(_inproj_body, tiles_per_seq=seq // tm),
        grid=(n_tok // tm,),
        in_specs=[pl.BlockSpec((tm, d), row)] + [_resident(a.shape) for a in vec_args],
        out_specs=out_specs,
        out_shape=out_shape,
        scratch_shapes=[pltpu.VMEM((tm + SUBLANES, PROJ_COLS), F32),
                        pltpu.VMEM((SUBLANES, d), F32),
                        pltpu.VMEM((SUBLANES, conv_ch), F32)],
        compiler_params=_params("arbitrary"),
        name="inproj",
    )(h, *vec_args)


def _split3(v):
    hi = v.astype(BF16)
    r1 = v - hi.astype(F32)
    mid = r1.astype(BF16)
    lo = (r1 - mid.astype(F32)).astype(BF16)
    return jnp.concatenate([hi, mid, lo], axis=1)


def _sum3(v):
    n = v.shape[1] // 3
    return v[:, :n] + v[:, n:2 * n] + v[:, 2 * n:]


def _lane_bcast(m, col, width):
    return jnp.broadcast_to(m[:, col:col + 1], (m.shape[0], width))


def _mixer_body(ya_ref, sz_ref, xs_ref, ga_ref, gb_ref, dt_ref, h_ref,
                alog_ref, dch_ref, ng_ref, expand_ref, woa_ref, wos_ref, wmx_ref, o_ref,
                state_ref, la_ref, dtx_ref, y_ref, *, d_inner, n_groups, n_state):
    t_tile = h_ref.shape[0]
    chunk = SSM_CHUNK
    gw = d_inner // n_groups
    pairs = gw // LANES
    hpg = gw // SSM_HEAD_DIM
    b_off, c_off = d_inner, d_inner + n_groups * n_state

    @pl.when(pl.program_id(1) == 0)
    def _():
        state_ref[...] = jnp.zeros_like(state_ref)

    dt_all = dt_ref[...]
    la_ref[...] = dt_all * (-jnp.exp(alog_ref[...]) * LOG2E)
    dtx_ref[...] = _dot(_split3(dt_all), expand_ref[...])

    row_i = lax.broadcasted_iota(jnp.int32, (chunk, chunk), 0)
    col_i = lax.broadcasted_iota(jnp.int32, (chunk, chunk), 1)
    causal = row_i >= col_i
    tril = causal.astype(BF16)
    low_half = col_i < SSM_HEAD_DIM

    def chunk_step(c, _):
        r0 = pl.multiple_of(c * chunk, chunk)
        rows = pl.ds(r0, chunk)
        acs = _sum3(_dot(tril, _split3(la_ref[rows, :])))
        acs_t = acs.T
        for g in range(n_groups):
            bm = xs_ref[rows, b_off + g * n_state:b_off + (g + 1) * n_state]
            cm = xs_ref[rows, c_off + g * n_state:c_off + (g + 1) * n_state]
            cb = _dot_nt(cm, bm)
            state = state_ref[g]
            y_off = _dot(cm, state.astype(BF16))
            xstate, decay_end = [], []
            for q in range(pairs):
                h0 = g * hpg + 2 * q
                ch = slice(g * gw + q * LANES, g * gw + (q + 1) * LANES)
                a0 = _lane_bcast(acs, h0, chunk)
                a1 = _lane_bcast(acs, h0 + 1, chunk)
                m0 = cb * jnp.where(causal, jnp.exp2(a0 - acs_t[h0:h0 + 1, :]), 0.0)
                m1 = cb * jnp.where(causal, jnp.exp2(a1 - acs_t[h0 + 1:h0 + 2, :]), 0.0)
                a_pair = jnp.where(low_half, a0, a1)
                xp = xs_ref[rows, ch].astype(F32)
                xdt = xp * dtx_ref[rows, ch]
                m01 = jnp.concatenate([m0.astype(BF16), m1.astype(BF16)], axis=1)
                x01 = jnp.concatenate([jnp.where(low_half, xdt, 0.0).astype(BF16),
                                       jnp.where(low_half, 0.0, xdt).astype(BF16)], axis=0)
                y_ref[rows, ch] = (_dot(m01, x01)
                                   + y_off[:, q * LANES:(q + 1) * LANES] * jnp.exp2(a_pair)
                                   + dch_ref[:, ch] * xp)
                total = a_pair[chunk - 1:chunk, :]
                xstate.append((xdt * jnp.exp2(total - a_pair)).astype(BF16))
                decay_end.append(jnp.exp2(total))
            xstate = jnp.concatenate(xstate, axis=1)
            decay_end = jnp.concatenate(decay_end, axis=1)
            state_ref[g] = state * decay_end + _dot_tn(bm, xstate)
        return 0

    lax.fori_loop(0, t_tile // chunk, chunk_step, 0)

    y_a = _dot(ya_ref[...], woa_ref[...])
    yn = []
    for g in range(n_groups):
        ch = slice(g * gw, (g + 1) * gw)
        yg = y_ref[:, ch] * sz_ref[:, ch].astype(F32)
        ms = jnp.mean(yg * yg, axis=-1, keepdims=True)
        yn.append((yg * lax.rsqrt(ms + EPS) * ng_ref[:, ch]).astype(BF16))
    y_b = _dot(jnp.concatenate(yn, axis=1), wos_ref[...])
    merged = ga_ref[...].astype(F32) * y_a + gb_ref[...].astype(F32) * y_b
    o_ref[...] = h_ref[...] + _dot(merged.astype(BF16), wmx_ref[...])


def _mixer(h, proj, a_log_row, d_skip, norm_g, w_out_a, w_out_ssm, w_mix_out, batch, seq):
    ya, sz, xs, ga, gb, dt = proj
    n_tok, d = h.shape
    d_inner = sz.shape[1]
    n_heads = d_inner // SSM_HEAD_DIM
    t = MIX_TOKENS
    tiles = seq // t
    row = lambda b, j: (b * tiles + j, 0)
    head_of = jnp.arange(d_inner) // SSM_HEAD_DIM
    expand = (jnp.arange(LANES)[:, None] == head_of[None, :]).astype(BF16)
    vec_args = [a_log_row,
                jnp.repeat(d_skip.astype(F32), SSM_HEAD_DIM).reshape(1, d_inner),
                norm_g.reshape(1, d_inner).astype(F32),
                jnp.concatenate([expand] * 3, axis=0)]
    w_args = [w_out_a.astype(BF16), w_out_ssm.astype(BF16), w_mix_out.astype(BF16)]
    tok_args = [ya, sz, xs, ga, gb, dt, h]
    in_specs = ([pl.BlockSpec((t, a.shape[1]), row) for a in tok_args]
                + [_resident(a.shape) for a in vec_args + w_args])
    scratch = [
        pltpu.VMEM((SSM_GROUPS, SSM_STATE, d_inner // SSM_GROUPS), F32),
        pltpu.VMEM((t, LANES), F32),
        pltpu.VMEM((t, d_inner), F32),
        pltpu.VMEM((t, d_inner), F32),
    ]
    return pl.pallas_call(
        functools.partial(_mixer_body, d_inner=d_inner, n_groups=SSM_GROUPS,
                          n_state=SSM_STATE),
        grid=(batch, tiles),
        in_specs=in_specs,
        out_specs=pl.BlockSpec((t, d), row),
        out_shape=jax.ShapeDtypeStruct((n_tok, d), F32),
        scratch_shapes=scratch,
        compiler_params=_params("parallel", "arbitrary"),
        name="mixer",
    )(*tok_args, *vec_args, *w_args)


def _kv_body(m_ref, g_ref, w_ref, o_ref):
    u = _rms(m_ref[...], g_ref[...]).astype(BF16)
    o_ref[...] = _dot(u, w_ref[...]).astype(BF16)


def _kv_proj(mem2d, norm_g, w_kv):
    n_tok, d = mem2d.shape
    tm = KV_TOKENS
    return pl.pallas_call(
        _kv_body,
        grid=(n_tok // tm,),
        in_specs=[pl.BlockSpec((tm, d), lambda i: (i, 0)), _resident((1, d)),
                  _resident(w_kv.shape)],
        out_specs=pl.BlockSpec((tm, w_kv.shape[1]), lambda i: (i, 0)),
        out_shape=jax.ShapeDtypeStruct((n_tok, w_kv.shape[1]), BF16),
        compiler_params=_params("parallel"),
        name="kv_proj",
    )(mem2d, norm_g.reshape(1, d), w_kv.astype(BF16))


def _xattn_body(h_ref, g_ref, wq_ref, k_ref, v_ref, wo_ref, o_ref, *, heads):
    x = h_ref[...]
    d = x.shape[1]
    dh = d // heads
    scale = 1.0 / math.sqrt(dh)
    u = _rms(x, g_ref[...]).astype(BF16)
    q = _dot(u, wq_ref[...]).astype(BF16)
    outs = []
    for hd in range(heads):
        ch = slice(hd * dh, (hd + 1) * dh)
        s = _dot_nt(q[:, ch], k_ref[0, :, ch]) * scale
        p = jnp.exp(s - jnp.max(s, axis=-1, keepdims=True))
        p = p / jnp.sum(p, axis=-1, keepdims=True)
        outs.append(_dot(p.astype(BF16), v_ref[0, :, ch]).astype(BF16))
    o_ref[...] = x + _dot(jnp.concatenate(outs, axis=1), wo_ref[...])


def _xattn(h, kv, norm_g, w_q, w_o, batch, seq):
    n_tok, d = h.shape
    m = kv.shape[0] // batch
    kv3 = kv.reshape(batch, m, 2 * d)
    tm = ATTN_TOKENS
    tiles = seq // tm
    row = lambda b, j: (b * tiles + j, 0)
    return pl.pallas_call(
        functools.partial(_xattn_body, heads=XATTN_HEADS),
        grid=(batch, tiles),
        in_specs=[pl.BlockSpec((tm, d), row), _resident((1, d)), _resident((d, d)),
                  pl.BlockSpec((1, m, d), lambda b, j: (b, 0, 0)),
                  pl.BlockSpec((1, m, d), lambda b, j: (b, 0, 1)),
                  _resident((d, d))],
        out_specs=pl.BlockSpec((tm, d), row),
        out_shape=jax.ShapeDtypeStruct((n_tok, d), F32),
        compiler_params=_params("parallel", "parallel"),
        name="xattn",
    )(h, norm_g.reshape(1, d), w_q.astype(BF16), kv3, kv3, w_o.astype(BF16))


def _split_w_in(w_in, d, d_inner, conv_ch, n_heads):
    sizes = (d, d, d, d_inner, conv_ch, n_heads, d, d)
    offs = [0]
    for n in sizes:
        offs.append(offs[-1] + n)
    piece = lambda i: w_in[:, offs[i]:offs[i + 1]]
    main = jnp.concatenate([piece(i) for i in (0, 1, 2, 3, 4, 6, 7)], axis=1).astype(BF16)
    w_dt = jnp.pad(piece(5), ((0, 0), (0, LANES - n_heads))).astype(BF16)
    return main, w_dt


def kernel(x, mem, ffn1_norm, ffn1_w_gate_up, ffn1_w_down, mix_norm, w_in, conv_a_w, w_out_a, ssm_conv_w, ssm_conv_b, ssm_dt_bias, ssm_a_log, ssm_d, ssm_norm, w_out_ssm, w_mix_out, xattn_norm, mem_norm, w_q, w_kv, w_o_x, ffn2_norm, ffn2_w_gate_up, ffn2_w_down, final_norm):
    batch, seq, d = x.shape
    depth = w_in.shape[0]
    d_inner = w_out_ssm.shape[1]
    conv_ch = ssm_conv_w.shape[2]
    n_heads = ssm_d.shape[1]
    h = x.reshape(batch * seq, d)
    mem2d = mem.reshape(-1, d)

    def head_row(v):
        return jnp.pad(v.astype(F32), (0, LANES - n_heads)).reshape(1, LANES)

    for i in range(depth):
        last = i == depth - 1
        h = _ffn(h, ffn1_norm[i], ffn1_w_gate_up[i], ffn1_w_down[i])
        w_main, w_dt = _split_w_in(w_in[i], d, d_inner, conv_ch, n_heads)
        proj = _inproj(h, mix_norm[i], w_main, w_dt, conv_a_w[i], ssm_conv_w[i],
                       ssm_conv_b[i], head_row(ssm_dt_bias[i]), d_inner, seq)
        h = _mixer(h, proj, head_row(ssm_a_log[i]), ssm_d[i], ssm_norm[i], w_out_a[i],
                   w_out_ssm[i], w_mix_out[i], batch, seq)
        kv = _kv_proj(mem2d, mem_norm[i], w_kv[i])
        h = _xattn(h, kv, xattn_norm[i], w_q[i], w_o_x[i], batch, seq)
        h = _ffn(h, ffn2_norm[i], ffn2_w_gate_up[i], ffn2_w_down[i],
                 final_g=final_norm if last else None)
    return h.reshape(batch, seq, d)
```

```python
import functools
import math

import jax
import jax.numpy as jnp
from jax import lax
from jax.experimental import pallas as pl
from jax.experimental.pallas import tpu as pltpu

F32 = jnp.float32
BF16 = jnp.bfloat16

EPS = 1e-6
FFN_RES_WEIGHT = 0.5
SSM_HEAD_DIM = 64
SSM_GROUPS = 4
SSM_STATE = 128
SSM_CHUNK = 128
XATTN_HEADS = 4
LOG2E = math.log2(math.e)

LANES = 128
SUBLANES = 8
MXU_DIM = 256
VMEM_LIMIT_BYTES = 60 * 1024 * 1024

FFN_TOKENS = 512
PROJ_TOKENS = 512
PROJ_COLS = 512
MIX_TOKENS = 512
ATTN_TOKENS = 512
KV_TOKENS = 256


def _dot(a, b):
    return jnp.dot(a, b, preferred_element_type=F32)


def _dot_nt(a, b):
    return lax.dot_general(a, b, (((1,), (1,)), ((), ())), preferred_element_type=F32)


def _dot_tn(a, b):
    return lax.dot_general(a, b, (((0,), (0,)), ((), ())), preferred_element_type=F32)


def _rms(x, g):
    ms = jnp.mean(x * x, axis=-1, keepdims=True)
    return x * lax.rsqrt(ms + EPS) * g


def _silu(x):
    return x * jax.nn.sigmoid(x)


def _resident(shape):
    zeros = (0,) * len(shape)
    return pl.BlockSpec(shape, lambda *_: zeros, pipeline_mode=pl.Buffered(1))


def _params(*semantics, flags=None):
    return pltpu.CompilerParams(dimension_semantics=semantics,
                                vmem_limit_bytes=VMEM_LIMIT_BYTES, flags=flags)


def _ff_chunks(d_ff):
    chunks, start = [], 0
    while start < d_ff:
        n = min(4 * MXU_DIM, d_ff - start)
        chunks.append((start, n))
        start += n
    return tuple(chunks)


def _ffn_body(x_ref, g_ref, wg_ref, wu_ref, wd_ref, *rest, chunks, final):
    if final:
        fg_ref, o_ref = rest
    else:
        (o_ref,) = rest
    x = x_ref[...]
    u = _rms(x, g_ref[...]).astype(BF16)
    acc = None
    for start, n in chunks:
        gate = _dot(u, wg_ref[:, start:start + n])
        up = _dot(u, wu_ref[:, start:start + n])
        act = (_silu(gate) * up).astype(BF16)
        part = _dot(act, wd_ref[start:start + n, :])
        acc = part if acc is None else acc + part
    h = x + FFN_RES_WEIGHT * acc
    if final:
        h = _rms(h, fg_ref[...])
    o_ref[...] = h


def _ffn(x, norm_g, w_gate_up, w_down, final_g=None):
    n_tok, d = x.shape
    d_ff = w_down.shape[0]
    wg = w_gate_up[:, :d_ff].astype(BF16)
    wu = w_gate_up[:, d_ff:].astype(BF16)
    wd = w_down.astype(BF16)
    tm = FFN_TOKENS
    tile = pl.BlockSpec((tm, d), lambda i: (i, 0))
    in_specs = [tile, _resident((1, d)), _resident((d, d_ff)), _resident((d, d_ff)),
                _resident((d_ff, d))]
    args = [x, norm_g.reshape(1, d), wg, wu, wd]
    if final_g is not None:
        in_specs.append(_resident((1, d)))
        args.append(final_g.reshape(1, d))
    return pl.pallas_call(
        functools.partial(_ffn_body, chunks=_ff_chunks(d_ff), final=final_g is not None),
        grid=(n_tok // tm,),
        in_specs=in_specs,
        out_specs=tile,
        out_shape=jax.ShapeDtypeStruct((n_tok, d), F32),
        compiler_params=_params("parallel"),
        name="ffn_final" if final_g is not None else "ffn",
    )(*args)


RING_SLOTS = 3


def _conv_rows(ring_ref, slot, tm, taps):
    k_taps = taps.shape[0]
    acc = None
    for back in range(k_taps):
        shifted = ring_ref[slot, SUBLANES - back:SUBLANES - back + tm, :]
        term = shifted * taps[k_taps - 1 - back:k_taps - back, :]
        acc = term if acc is None else acc + term
    return acc


def _inproj_body(h_ref, g_ref, wm_ref, wdt_ref, caw_ref, csw_ref, csb_ref, dtb_ref,
                 ya_ref, sz_ref, xs_ref, ga_ref, gb_ref, dt_ref,
                 ring_ref, side_ref, halo_c_ref, halo_x_ref, *, tiles_per_seq):
    tm, d = h_ref.shape
    d_inner = sz_ref.shape[1]
    conv_ch = xs_ref.shape[1]
    cw = PROJ_COLS

    @pl.when(pl.program_id(0) % tiles_per_seq == 0)
    def _():
        halo_c_ref[...] = jnp.zeros_like(halo_c_ref)
        halo_x_ref[...] = jnp.zeros_like(halo_x_ref)

    u = _rms(h_ref[...], g_ref[...]).astype(BF16)

    def proj(off):
        return _dot(u, wm_ref[:, off:off + cw])

    def park(slot, x, halo_ref=None, cols=None):
        if halo_ref is not None:
            ring_ref[slot, 0:SUBLANES, :] = halo_ref[:, cols]
            halo_ref[:, cols] = x[tm - SUBLANES:, :]
        ring_ref[slot, SUBLANES:SUBLANES + tm, :] = x

    z_off, x_off, g_off = 3 * d, 3 * d + d_inner, 3 * d + d_inner + conv_ch

    def conv3_job(s):
        cols = slice(s, s + cw)

        def front(slot):
            park(slot, proj(d + s) * proj(2 * d + s), halo_c_ref, cols)
            side_ref[slot] = proj(s)

        def back(slot):
            conv = _conv_rows(ring_ref, slot, tm, caw_ref[:, cols])
            ya_ref[:, cols] = (side_ref[slot] * conv).astype(BF16)
        return front, back

    def conv4_job(s):
        cols = slice(s, s + cw)

        def front(slot):
            park(slot, proj(x_off + s), halo_x_ref, cols)

        def back(slot):
            conv = _conv_rows(ring_ref, slot, tm, csw_ref[:, cols])
            xs_ref[:, cols] = _silu(conv + csb_ref[:, cols]).astype(BF16)
        return front, back

    def gate_job(o_ref, off, s, fn):
        def front(slot):
            park(slot, proj(off + s))

        def back(slot):
            o_ref[:, s:s + cw] = fn(ring_ref[slot, SUBLANES:SUBLANES + tm, :]).astype(BF16)
        return front, back

    heavy = [conv4_job(s) for s in range(0, conv_ch, cw)]
    light = ([gate_job(sz_ref, z_off, s, _silu) for s in range(0, d_inner, cw)]
             + [conv3_job(s) for s in range(0, d, cw)]
             + [gate_job(ga_ref, g_off, s, jax.nn.sigmoid) for s in range(0, d, cw)]
             + [gate_job(gb_ref, g_off + d, s, jax.nn.sigmoid) for s in range(0, d, cw)])
    jobs = []
    while heavy or light:
        if heavy:
            jobs.append(heavy.pop(0))
        if light:
            jobs.append(light.pop(0))
    pending = None
    for j, (front, back) in enumerate(jobs):
        front(j % RING_SLOTS)
        if pending is not None:
            pending()
        pending = functools.partial(back, j % RING_SLOTS)
    dt_ref[...] = jax.nn.softplus(_dot(u, wdt_ref[...]) + dtb_ref[...])
    pending()


def _inproj(h, norm_g, w_main, w_dt, conv_a_w, ssm_conv_w, ssm_conv_b, dt_bias_row,
            d_inner, seq):
    n_tok, d = h.shape
    conv_ch = ssm_conv_w.shape[1]
    tm = PROJ_TOKENS
    row = lambda i: (i, 0)
    widths = (d, d_inner, conv_ch, d, d)
    out_shape = [jax.ShapeDtypeStruct((n_tok, n), BF16) for n in widths]
    out_shape.append(jax.ShapeDtypeStruct((n_tok, LANES), F32))
    out_specs = [pl.BlockSpec((tm, n), row) for n in widths]
    out_specs.append(pl.BlockSpec((tm, LANES), row))
    vec_args = [norm_g.reshape(1, d), w_main, w_dt, conv_a_w.astype(F32),
                ssm_conv_w.astype(F32), ssm_conv_b.reshape(1, conv_ch).astype(F32),
                dt_bias_row]
    return pl.pallas_call(
        functools.partial(_inproj_body, tiles_per_seq=seq // tm),
        grid=(n_tok // tm,),
        in_specs=[pl.BlockSpec((tm, d), row)] + [_resident(a.shape) for a in vec_args],
        out_specs=out_specs,
        out_shape=out_shape,
        scratch_shapes=[pltpu.VMEM((RING_SLOTS, tm + SUBLANES, PROJ_COLS), F32),
                        pltpu.VMEM((RING_SLOTS, tm, PROJ_COLS), F32),
                        pltpu.VMEM((SUBLANES, d), F32),
                        pltpu.VMEM((SUBLANES, conv_ch), F32)],
        compiler_params=_params("arbitrary"),
        name="inproj",
    )(h, *vec_args)


def _split3(v):
    hi = v.astype(BF16)
    r1 = v - hi.astype(F32)
    mid = r1.astype(BF16)
    lo = (r1 - mid.astype(F32)).astype(BF16)
    return jnp.concatenate([hi, mid, lo], axis=1)


def _sum3(v):
    n = v.shape[1] // 3
    return v[:, :n] + v[:, n:2 * n] + v[:, 2 * n:]


def _lane_bcast(m, col, width):
    return jnp.broadcast_to(m[:, col:col + 1], (m.shape[0], width))


def _proj_pieces(rows, yn_ref, ya_ref, ga_ref, gb_ref, h_ref, woa_ref, wos_ref, wmx_ref,
                 o_ref, pyb_ref, pya_ref, mg_ref):
    d = h_ref.shape[1]
    n_tiles = d // MXU_DIM
    cols = [slice(n * MXU_DIM, (n + 1) * MXU_DIM) for n in range(n_tiles)]

    def merge(n):
        mg_ref[:, cols[n]] = (ga_ref[rows, cols[n]].astype(F32) * pya_ref[n]
                              + gb_ref[rows, cols[n]].astype(F32) * pyb_ref[n]).astype(BF16)

    def yb_piece(n):
        def run():
            pyb_ref[n] = _dot(yn_ref[rows, :], wos_ref[:, cols[n]])
        return run

    def ya_piece(n):
        def run():
            pya_ref[n] = _dot(ya_ref[rows, :], woa_ref[:, cols[n]])
            if n > 0:
                merge(n - 1)
        return run

    def mix_piece(n):
        def run():
            if n == 0:
                merge(n_tiles - 1)
            o_ref[rows, cols[n]] = h_ref[rows, cols[n]] + _dot(mg_ref[...], wmx_ref[:, cols[n]])
        return run

    return ([yb_piece(n) for n in range(n_tiles)] + [ya_piece(n) for n in range(n_tiles)]
            + [mix_piece(n) for n in range(n_tiles)])


def _mixer_body(ya_ref, ga_ref, gb_ref, h_ref, sz_ref, xs_ref, dt_ref,
                alog_ref, dch_ref, ng_ref, expand_ref, woa_ref, wos_ref, wmx_ref, o_ref,
                state_ref, yn_ref, dtx_ref, grp_ref, pair_ref, tail_ref, yg_ref,
                pyb_ref, pya_ref, mg_ref, *, d_inner, n_groups, n_state, tiles_per_seq):
    t_tile = h_ref.shape[0]
    chunk = SSM_CHUNK
    gw = d_inner // n_groups
    pairs = gw // LANES
    hpg = gw // SSM_HEAD_DIM
    b_off, c_off = d_inner, d_inner + n_groups * n_state
    assert h_ref.shape[1] // MXU_DIM <= pairs
    step = pl.program_id(0)

    @pl.when(step == 0)
    def _():
        yn_ref[...] = jnp.zeros_like(yn_ref)

    @pl.when(step % tiles_per_seq == 0)
    def _():
        state_ref[...] = jnp.zeros_like(state_ref)

    a_log2 = -jnp.exp(alog_ref[...]) * LOG2E

    row_i = lax.broadcasted_iota(jnp.int32, (chunk, chunk), 0)
    col_i = lax.broadcasted_iota(jnp.int32, (chunk, chunk), 1)
    causal = row_i >= col_i
    tril = causal.astype(BF16)
    low_half = col_i < SSM_HEAD_DIM

    for c in range(t_tile // chunk):
        rows = slice(c * chunk, (c + 1) * chunk)
        pieces = iter(_proj_pieces(rows, yn_ref, ya_ref, ga_ref, gb_ref, h_ref, woa_ref,
                                   wos_ref, wmx_ref, o_ref, pyb_ref, pya_ref, mg_ref))

        dt_c = dt_ref[rows, :]
        dtx_ref[...] = _dot(_split3(dt_c), expand_ref[...])
        acs = _sum3(_dot(tril, _split3(dt_c * a_log2)))
        acs_t = acs.T

        def b_of(g):
            return xs_ref[rows, b_off + g * n_state:b_off + (g + 1) * n_state]

        def group_front(g):
            cm = xs_ref[rows, c_off + g * n_state:c_off + (g + 1) * n_state]
            grp_ref[g % 2, :, 0:chunk] = _dot_nt(cm, b_of(g))
            grp_ref[g % 2, :, chunk:] = _dot(cm, state_ref[g].astype(BF16))

        def unit_front(g, q):
            slot = (g * pairs + q) % 2
            h0 = g * hpg + 2 * q
            ch = slice(g * gw + q * LANES, g * gw + (q + 1) * LANES)
            cb = grp_ref[g % 2, :, 0:chunk]
            a0 = _lane_bcast(acs, h0, chunk)
            a1 = _lane_bcast(acs, h0 + 1, chunk)
            m0 = cb * jnp.where(causal, jnp.exp2(a0 - acs_t[h0:h0 + 1, :]), 0.0)
            m1 = cb * jnp.where(causal, jnp.exp2(a1 - acs_t[h0 + 1:h0 + 2, :]), 0.0)
            a_pair = jnp.where(low_half, a0, a1)
            xp = xs_ref[rows, ch].astype(F32)
            xdt = xp * dtx_ref[:, ch]
            m01 = jnp.concatenate([m0.astype(BF16), m1.astype(BF16)], axis=1)
            x01 = jnp.concatenate([jnp.where(low_half, xdt, 0.0).astype(BF16),
                                   jnp.where(low_half, 0.0, xdt).astype(BF16)], axis=0)
            pair_ref[slot] = _dot(m01, x01)
            y_off = grp_ref[g % 2, :, chunk + q * LANES:chunk + (q + 1) * LANES]
            tail_ref[slot] = y_off * jnp.exp2(a_pair) + dch_ref[:, ch] * xp
            total = a_pair[chunk - 1:chunk, :]
            return (xdt * jnp.exp2(total - a_pair)).astype(BF16), jnp.exp2(total)

        def unit_back(g, q):
            slot = (g * pairs + q) % 2
            ch = slice(g * gw + q * LANES, g * gw + (q + 1) * LANES)
            y = pair_ref[slot] + tail_ref[slot]
            yg_ref[g % 2, :, q * LANES:(q + 1) * LANES] = y * sz_ref[rows, ch].astype(F32)

        def group_back(g):
            y_g = yg_ref[g % 2]
            ms = jnp.mean(y_g * y_g, axis=-1, keepdims=True)
            gch = slice(g * gw, (g + 1) * gw)
            yn_ref[rows, gch] = (y_g * lax.rsqrt(ms + EPS) * ng_ref[:, gch]).astype(BF16)

        group_front(0)
        backs = []
        for g in range(n_groups):
            if g + 1 < n_groups:
                group_front(g + 1)
            xstate, decay_end = [], []
            for q in range(pairs):
                next(pieces, lambda: None)()
                xs_q, de_q = unit_front(g, q)
                xstate.append(xs_q)
                decay_end.append(de_q)
                for back in backs:
                    back()
                backs = [functools.partial(unit_back, g, q)]
                if q == pairs - 1:
                    backs.append(functools.partial(group_back, g))
            xstate = jnp.concatenate(xstate, axis=1)
            decay_end = jnp.concatenate(decay_end, axis=1)
            state_ref[g] = state_ref[g] * decay_end + _dot_tn(b_of(g), xstate)
        for back in backs:
            back()
        for piece in pieces:
            piece()


def _mixer(h, proj, a_log_row, d_skip, norm_g, w_out_a, w_out_ssm, w_mix_out, seq):
    ya, sz, xs, ga, gb, dt = proj
    n_tok, d = h.shape
    d_inner = sz.shape[1]
    t = MIX_TOKENS
    n_tiles = n_tok // t
    ssd_row = lambda i: (jnp.minimum(i, n_tiles - 1), 0)
    proj_row = lambda i: (jnp.maximum(i - 1, 0), 0)
    head_of = jnp.arange(d_inner) // SSM_HEAD_DIM
    expand = (jnp.arange(LANES)[:, None] == head_of[None, :]).astype(BF16)
    vec_args = [a_log_row,
                jnp.repeat(d_skip.astype(F32), SSM_HEAD_DIM).reshape(1, d_inner),
                norm_g.reshape(1, d_inner).astype(F32),
                jnp.concatenate([expand] * 3, axis=0)]
    w_args = [w_out_a.astype(BF16), w_out_ssm.astype(BF16), w_mix_out.astype(BF16)]
    proj_args = [ya, ga, gb, h]
    ssd_args = [sz, xs, dt]
    in_specs = ([pl.BlockSpec((t, a.shape[1]), proj_row) for a in proj_args]
                + [pl.BlockSpec((t, a.shape[1]), ssd_row) for a in ssd_args]
                + [_resident(a.shape) for a in vec_args + w_args])
    scratch = [
        pltpu.VMEM((SSM_GROUPS, SSM_STATE, d_inner // SSM_GROUPS), F32),
        pltpu.VMEM((t, d_inner), BF16),
        pltpu.VMEM((SSM_CHUNK, d_inner), F32),
        pltpu.VMEM((2, SSM_CHUNK, SSM_CHUNK + d_inner // SSM_GROUPS), F32),
        pltpu.VMEM((2, SSM_CHUNK, LANES), F32),
        pltpu.VMEM((2, SSM_CHUNK, LANES), F32),
        pltpu.VMEM((2, SSM_CHUNK, d_inner // SSM_GROUPS), F32),
        pltpu.VMEM((d // MXU_DIM, SSM_CHUNK, MXU_DIM), F32),
        pltpu.VMEM((d // MXU_DIM, SSM_CHUNK, MXU_DIM), F32),
        pltpu.VMEM((SSM_CHUNK, d), BF16),
    ]
    return pl.pallas_call(
        functools.partial(_mixer_body, d_inner=d_inner, n_groups=SSM_GROUPS,
                          n_state=SSM_STATE, tiles_per_seq=seq // t),
        grid=(n_tiles + 1,),
        in_specs=in_specs,
        out_specs=pl.BlockSpec((t, d), proj_row),
        out_shape=jax.ShapeDtypeStruct((n_tok, d), F32),
        scratch_shapes=scratch,
        compiler_params=_params("arbitrary"),
        name="mixer",
    )(*proj_args, *ssd_args, *vec_args, *w_args)


def _kv_body(m_ref, g_ref, w_ref, o_ref):
    u = _rms(m_ref[...], g_ref[...]).astype(BF16)
    o_ref[...] = _dot(u, w_ref[...]).astype(BF16)


def _kv_proj(mem2d, norm_g, w_kv):
    n_tok, d = mem2d.shape
    tm = KV_TOKENS
    return pl.pallas_call(
        _kv_body,
        grid=(n_tok // tm,),
        in_specs=[pl.BlockSpec((tm, d), lambda i: (i, 0)), _resident((1, d)),
                  _resident(w_kv.shape)],
        out_specs=pl.BlockSpec((tm, w_kv.shape[1]), lambda i: (i, 0)),
        out_shape=jax.ShapeDtypeStruct((n_tok, w_kv.shape[1]), BF16),
        compiler_params=_params("parallel"),
        name="kv_proj",
    )(mem2d, norm_g.reshape(1, d), w_kv.astype(BF16))


def _xattn_body(h_ref, g_ref, wq_ref, k_ref, v_ref, wo_ref, o_ref, *, heads):
    x = h_ref[...]
    d = x.shape[1]
    dh = d // heads
    scale = 1.0 / math.sqrt(dh)
    u = _rms(x, g_ref[...]).astype(BF16)
    q = _dot(u, wq_ref[...]).astype(BF16)
    outs = []
    for hd in range(heads):
        ch = slice(hd * dh, (hd + 1) * dh)
        s = _dot_nt(q[:, ch], k_ref[0, :, ch]) * scale
        p = jnp.exp(s - jnp.max(s, axis=-1, keepdims=True))
        p = p / jnp.sum(p, axis=-1, keepdims=True)
        outs.append(_dot(p.astype(BF16), v_ref[0, :, ch]).astype(BF16))
    o_ref[...] = x + _dot(jnp.concatenate(outs, axis=1), wo_ref[...])


def _xattn(h, kv, norm_g, w_q, w_o, batch, seq):
    n_tok, d = h.shape
    m = kv.shape[0] // batch
    kv3 = kv.reshape(batch, m, 2 * d)
    tm = ATTN_TOKENS
    tiles = seq // tm
    row = lambda b, j: (b * tiles + j, 0)
    return pl.pallas_call(
        functools.partial(_xattn_body, heads=XATTN_HEADS),
        grid=(batch, tiles),
        in_specs=[pl.BlockSpec((tm, d), row), _resident((1, d)), _resident((d, d)),
                  pl.BlockSpec((1, m, d), lambda b, j: (b, 0, 0)),
                  pl.BlockSpec((1, m, d), lambda b, j: (b, 0, 1)),
                  _resident((d, d))],
        out_specs=pl.BlockSpec((tm, d), row),
        out_shape=jax.ShapeDtypeStruct((n_tok, d), F32),
        compiler_params=_params("parallel", "parallel"),
        name="xattn",
    )(h, norm_g.reshape(1, d), w_q.astype(BF16), kv3, kv3, w_o.astype(BF16))


def _split_w_in(w_in, d, d_inner, conv_ch, n_heads):
    sizes = (d, d, d, d_inner, conv_ch, n_heads, d, d)
    offs = [0]
    for n in sizes:
        offs.append(offs[-1] + n)
    piece = lambda i: w_in[:, offs[i]:offs[i + 1]]
    main = jnp.concatenate([piece(i) for i in (0, 1, 2, 3, 4, 6, 7)], axis=1).astype(BF16)
    w_dt = jnp.pad(piece(5), ((0, 0), (0, LANES - n_heads))).astype(BF16)
    return main, w_dt


def kernel(x, mem, ffn1_norm, ffn1_w_gate_up, ffn1_w_down, mix_norm, w_in, conv_a_w, w_out_a, ssm_conv_w, ssm_conv_b, ssm_dt_bias, ssm_a_log, ssm_d, ssm_norm, w_out_ssm, w_mix_out, xattn_norm, mem_norm, w_q, w_kv, w_o_x, ffn2_norm, ffn2_w_gate_up, ffn2_w_down, final_norm):
    batch, seq, d = x.shape
    depth = w_in.shape[0]
    d_inner = w_out_ssm.shape[1]
    conv_ch = ssm_conv_w.shape[2]
    n_heads = ssm_d.shape[1]
    h = x.reshape(batch * seq, d)
    mem2d = mem.reshape(-1, d)

    def head_row(v):
        return jnp.pad(v.astype(F32), (0, LANES - n_heads)).reshape(1, LANES)

    for i in range(depth):
        last = i == depth - 1
        h = _ffn(h, ffn1_norm[i], ffn1_w_gate_up[i], ffn1_w_down[i])
        w_main, w_dt = _split_w_in(w_in[i], d, d_inner, conv_ch, n_heads)
        proj = _inproj(h, mix_norm[i], w_main, w_dt, conv_a_w[i], ssm_conv_w[i],
                       ssm_conv_b[i], head_row(ssm_dt_bias[i]), d_inner, seq)
        h = _mixer(h, proj, head_row(ssm_a_log[i]), ssm_d[i], ssm_norm[i], w_out_a[i],
                   w_out_ssm[i], w_mix_out[i], seq)
        kv = _kv_proj(mem2d, mem_norm[i], w_kv[i])
        h = _xattn(h, kv, xattn_norm[i], w_q[i], w_o_x[i], batch, seq)
        h = _ffn(h, ffn2_norm[i], ffn2_w_gate_up[i], ffn2_w_down[i],
                 final_g=final_norm if last else None)
    return h.reshape(batch, seq, d)
```

```python
import functools
import math

import jax
import jax.numpy as jnp
from jax import lax
from jax.experimental import pallas as pl
from jax.experimental.pallas import tpu as pltpu

F32 = jnp.float32
BF16 = jnp.bfloat16

EPS = 1e-6
FFN_RES_WEIGHT = 0.5
SSM_HEAD_DIM = 64
SSM_GROUPS = 4
SSM_STATE = 128
SSM_CHUNK = 128
XATTN_HEADS = 4
LOG2E = math.log2(math.e)

LANES = 128
SUBLANES = 8
MXU_DIM = 256
VMEM_LIMIT_BYTES = 60 * 1024 * 1024

FFN_TOKENS = 1024
PROJ_TOKENS = 512
PROJ_COLS = 512
MIX_TOKENS = 512
ATTN_TOKENS = 512
KV_TOKENS = 256


def _dot(a, b):
    return jnp.dot(a, b, preferred_element_type=F32)


def _dot_nt(a, b):
    return lax.dot_general(a, b, (((1,), (1,)), ((), ())), preferred_element_type=F32)


def _dot_tn(a, b):
    return lax.dot_general(a, b, (((0,), (0,)), ((), ())), preferred_element_type=F32)


def _rms(x, g):
    ms = jnp.mean(x * x, axis=-1, keepdims=True)
    return x * lax.rsqrt(ms + EPS) * g


def _silu(x):
    return x * jax.nn.sigmoid(x)


def _resident(shape):
    zeros = (0,) * len(shape)
    return pl.BlockSpec(shape, lambda *_: zeros, pipeline_mode=pl.Buffered(1))


def _params(*semantics, flags=None):
    return pltpu.CompilerParams(dimension_semantics=semantics,
                                vmem_limit_bytes=VMEM_LIMIT_BYTES, flags=flags)


def _ff_chunks(d_ff):
    chunks, start = [], 0
    while start < d_ff:
        n = min(4 * MXU_DIM, d_ff - start)
        chunks.append((start, n))
        start += n
    return tuple(chunks)


def _ffn_body(x_ref, g_ref, wgu_ref, wd_ref, *rest, chunks, d_ff, final):
    if final:
        fg_ref, o_ref = rest
    else:
        (o_ref,) = rest
    x = x_ref[...]
    u = _rms(x, g_ref[...]).astype(BF16)
    acc = None
    for start, n in chunks:
        gate = _dot(u, wgu_ref[:, start:start + n])
        up = _dot(u, wgu_ref[:, d_ff + start:d_ff + start + n])
        act = (_silu(gate) * up).astype(BF16)
        part = _dot(act, wd_ref[start:start + n, :])
        acc = part if acc is None else acc + part
    h = x + FFN_RES_WEIGHT * acc
    if final:
        h = _rms(h, fg_ref[...])
    o_ref[...] = h


def _ffn(x, norm_g, w_gate_up, w_down, final_g=None):
    n_tok, d = x.shape
    d_ff = w_down.shape[0]
    tm = FFN_TOKENS
    tile = pl.BlockSpec((tm, d), lambda i: (i, 0))
    in_specs = [tile, _resident((1, d)), _resident((d, 2 * d_ff)), _resident((d_ff, d))]
    args = [x, norm_g.reshape(1, d), w_gate_up.astype(BF16), w_down.astype(BF16)]
    if final_g is not None:
        in_specs.append(_resident((1, d)))
        args.append(final_g.reshape(1, d))
    return pl.pallas_call(
        functools.partial(_ffn_body, chunks=_ff_chunks(d_ff), d_ff=d_ff,
                          final=final_g is not None),
        grid=(n_tok // tm,),
        in_specs=in_specs,
        out_specs=tile,
        out_shape=jax.ShapeDtypeStruct((n_tok, d), F32),
        compiler_params=_params("parallel"),
        name="ffn_final" if final_g is not None else "ffn",
    )(*args)


RING_SLOTS = 3


def _conv_rows(ring_ref, slot, tm, taps):
    k_taps = taps.shape[0]
    acc = None
    for back in range(k_taps):
        shifted = ring_ref[slot, SUBLANES - back:SUBLANES - back + tm, :]
        term = shifted * taps[k_taps - 1 - back:k_taps - back, :]
        acc = term if acc is None else acc + term
    return acc


def _inproj_body(h_ref, g_ref, wf_ref, wg_ref, wdt_ref, caw_ref, csw_ref, csb_ref, dtb_ref,
                 ya_ref, sz_ref, xs_ref, ga_ref, gb_ref, dt_ref,
                 ring_ref, side_ref, halo_c_ref, halo_x_ref, *, tiles_per_seq):
    tm, d = h_ref.shape
    d_inner = sz_ref.shape[1]
    conv_ch = xs_ref.shape[1]
    cw = PROJ_COLS

    @pl.when(pl.program_id(0) % tiles_per_seq == 0)
    def _():
        halo_c_ref[...] = jnp.zeros_like(halo_c_ref)
        halo_x_ref[...] = jnp.zeros_like(halo_x_ref)

    u = _rms(h_ref[...], g_ref[...]).astype(BF16)
    n_front = wf_ref.shape[1]

    def proj(off):
        if off < n_front:
            return _dot(u, wf_ref[:, off:off + cw])
        return _dot(u, wg_ref[:, off - n_front:off - n_front + cw])

    def park(slot, x, halo_ref=None, cols=None):
        if halo_ref is not None:
            ring_ref[slot, 0:SUBLANES, :] = halo_ref[:, cols]
            halo_ref[:, cols] = x[tm - SUBLANES:, :]
        ring_ref[slot, SUBLANES:SUBLANES + tm, :] = x

    z_off, x_off, g_off = 3 * d, 3 * d + d_inner, 3 * d + d_inner + conv_ch

    def conv3_job(s):
        cols = slice(s, s + cw)

        def front(slot):
            park(slot, proj(d + s) * proj(2 * d + s), halo_c_ref, cols)
            side_ref[slot] = proj(s)

        def back(slot):
            conv = _conv_rows(ring_ref, slot, tm, caw_ref[:, cols])
            ya_ref[:, cols] = (side_ref[slot] * conv).astype(BF16)
        return front, back

    def conv4_job(s):
        cols = slice(s, s + cw)

        def front(slot):
            park(slot, proj(x_off + s), halo_x_ref, cols)

        def back(slot):
            conv = _conv_rows(ring_ref, slot, tm, csw_ref[:, cols])
            xs_ref[:, cols] = _silu(conv + csb_ref[:, cols]).astype(BF16)
        return front, back

    def gate_job(o_ref, off, s, fn):
        def front(slot):
            park(slot, proj(off + s))

        def back(slot):
            o_ref[:, s:s + cw] = fn(ring_ref[slot, SUBLANES:SUBLANES + tm, :]).astype(BF16)
        return front, back

    heavy = [conv4_job(s) for s in range(0, conv_ch, cw)]
    light = ([gate_job(sz_ref, z_off, s, _silu) for s in range(0, d_inner, cw)]
             + [conv3_job(s) for s in range(0, d, cw)]
             + [gate_job(ga_ref, g_off, s, jax.nn.sigmoid) for s in range(0, d, cw)]
             + [gate_job(gb_ref, g_off + d, s, jax.nn.sigmoid) for s in range(0, d, cw)])
    jobs = []
    while heavy or light:
        if heavy:
            jobs.append(heavy.pop(0))
        if light:
            jobs.append(light.pop(0))
    pending = None
    for j, (front, back) in enumerate(jobs):
        front(j % RING_SLOTS)
        if pending is not None:
            pending()
        pending = functools.partial(back, j % RING_SLOTS)
    dt_ref[...] = jax.nn.softplus(_dot(u, wdt_ref[...]) + dtb_ref[...])
    pending()


def _inproj(h, norm_g, w_front, w_gates, w_dt, conv_a_w, ssm_conv_w, ssm_conv_b, dt_bias_row,
            d_inner, seq):
    n_tok, d = h.shape
    conv_ch = ssm_conv_w.shape[1]
    tm = PROJ_TOKENS
    row = lambda i: (i, 0)
    widths = (d, d_inner, conv_ch, d, d)
    out_shape = [jax.ShapeDtypeStruct((n_tok, n), BF16) for n in widths]
    out_shape.append(jax.ShapeDtypeStruct((n_tok, LANES), F32))
    out_specs = [pl.BlockSpec((tm, n), row) for n in widths]
    out_specs.append(pl.BlockSpec((tm, LANES), row))
    vec_args = [norm_g.reshape(1, d), w_front, w_gates, w_dt, conv_a_w.astype(F32),
                ssm_conv_w.astype(F32), ssm_conv_b.reshape(1, conv_ch).astype(F32),
                dt_bias_row]
    return pl.pallas_call(
        functools.partial(_inproj_body, tiles_per_seq=seq // tm),
        grid=(n_tok // tm,),
        in_specs=[pl.BlockSpec((tm, d), row)] + [_resident(a.shape) for a in vec_args],
        out_specs=out_specs,
        out_shape=out_shape,
        scratch_shapes=[pltpu.VMEM((RING_SLOTS, tm + SUBLANES, PROJ_COLS), F32),
                        pltpu.VMEM((RING_SLOTS, tm, PROJ_COLS), F32),
                        pltpu.VMEM((SUBLANES, d), F32),
                        pltpu.VMEM((SUBLANES, conv_ch), F32)],
        compiler_params=_params("arbitrary"),
        name="inproj",
    )(h, *vec_args)


def _split3(v):
    hi = v.astype(BF16)
    r1 = v - hi.astype(F32)
    mid = r1.astype(BF16)
    lo = (r1 - mid.astype(F32)).astype(BF16)
    return jnp.concatenate([hi, mid, lo], axis=1)


def _sum3(v):
    n = v.shape[1] // 3
    return v[:, :n] + v[:, n:2 * n] + v[:, 2 * n:]


def _lane_bcast(m, col, width):
    return jnp.broadcast_to(m[:, col:col + 1], (m.shape[0], width))


def _proj_pieces(rows, yn_ref, ya_ref, ga_ref, gb_ref, h_ref, woa_ref, wos_ref, wmx_ref,
                 o_ref, pyb_ref, pya_ref, mg_ref):
    d = h_ref.shape[1]
    n_tiles = d // MXU_DIM
    cols = [slice(n * MXU_DIM, (n + 1) * MXU_DIM) for n in range(n_tiles)]

    def merge(n):
        mg_ref[:, cols[n]] = (ga_ref[rows, cols[n]].astype(F32) * pya_ref[n]
                              + gb_ref[rows, cols[n]].astype(F32) * pyb_ref[n]).astype(BF16)

    def yb_piece(n):
        def run():
            pyb_ref[n] = _dot(yn_ref[rows, :], wos_ref[:, cols[n]])
        return run

    def ya_piece(n):
        def run():
            pya_ref[n] = _dot(ya_ref[rows, :], woa_ref[:, cols[n]])
            if n > 0:
                merge(n - 1)
        return run

    def mix_piece(n):
        def run():
            if n == 0:
                merge(n_tiles - 1)
            o_ref[rows, cols[n]] = h_ref[rows, cols[n]] + _dot(mg_ref[...], wmx_ref[:, cols[n]])
        return run

    return ([yb_piece(n) for n in range(n_tiles)] + [ya_piece(n) for n in range(n_tiles)]
            + [mix_piece(n) for n in range(n_tiles)])


def _mixer_body(ya_ref, ga_ref, gb_ref, h_ref, sz_ref, xs_ref, dt_ref,
                alog_ref, dch_ref, ng_ref, woa_ref, wos_ref, wmx_ref, o_ref,
                state_ref, yn_ref, grp_ref, pair_ref, tail_ref, yg_ref,
                pyb_ref, pya_ref, mg_ref, *, d_inner, n_groups, n_state, tiles_per_seq):
    t_tile = h_ref.shape[0]
    chunk = SSM_CHUNK
    gw = d_inner // n_groups
    pairs = gw // LANES
    hpg = gw // SSM_HEAD_DIM
    b_off, c_off = d_inner, d_inner + n_groups * n_state
    assert h_ref.shape[1] // MXU_DIM <= pairs
    step = pl.program_id(0)

    @pl.when(step == 0)
    def _():
        yn_ref[...] = jnp.zeros_like(yn_ref)

    @pl.when(step % tiles_per_seq == 0)
    def _():
        state_ref[...] = jnp.zeros_like(state_ref)

    a_log2 = -jnp.exp(alog_ref[...]) * LOG2E

    row_i = lax.broadcasted_iota(jnp.int32, (chunk, chunk), 0)
    col_i = lax.broadcasted_iota(jnp.int32, (chunk, chunk), 1)
    causal = row_i >= col_i
    tril = causal.astype(BF16)
    low_half = col_i < SSM_HEAD_DIM

    for c in range(t_tile // chunk):
        rows = slice(c * chunk, (c + 1) * chunk)
        pieces = iter(_proj_pieces(rows, yn_ref, ya_ref, ga_ref, gb_ref, h_ref, woa_ref,
                                   wos_ref, wmx_ref, o_ref, pyb_ref, pya_ref, mg_ref))

        dt_c = dt_ref[rows, :]
        acs = _sum3(_dot(tril, _split3(dt_c * a_log2)))
        acs_t = acs.T

        def b_of(g):
            return xs_ref[rows, b_off + g * n_state:b_off + (g + 1) * n_state]

        def group_front(g):
            cm = xs_ref[rows, c_off + g * n_state:c_off + (g + 1) * n_state]
            grp_ref[g % 2, :, 0:chunk] = _dot_nt(cm, b_of(g))
            grp_ref[g % 2, :, chunk:] = _dot(cm, state_ref[g].astype(BF16))

        def unit_front(g, q):
            slot = (g * pairs + q) % 2
            h0 = g * hpg + 2 * q
            ch = slice(g * gw + q * LANES, g * gw + (q + 1) * LANES)
            cb = grp_ref[g % 2, :, 0:chunk]
            a0 = _lane_bcast(acs, h0, chunk)
            a1 = _lane_bcast(acs, h0 + 1, chunk)
            m0 = cb * jnp.where(causal, jnp.exp2(a0 - acs_t[h0:h0 + 1, :]), 0.0)
            m1 = cb * jnp.where(causal, jnp.exp2(a1 - acs_t[h0 + 1:h0 + 2, :]), 0.0)
            a_pair = jnp.where(low_half, a0, a1)
            xp = xs_ref[rows, ch].astype(F32)
            xdt = xp * jnp.where(low_half, _lane_bcast(dt_c, h0, LANES),
                                 _lane_bcast(dt_c, h0 + 1, LANES))
            m01 = jnp.concatenate([m0.astype(BF16), m1.astype(BF16)], axis=1)
            x01 = jnp.concatenate([jnp.where(low_half, xdt, 0.0).astype(BF16),
                                   jnp.where(low_half, 0.0, xdt).astype(BF16)], axis=0)
            pair_ref[slot] = _dot(m01, x01)
            y_off = grp_ref[g % 2, :, chunk + q * LANES:chunk + (q + 1) * LANES]
            tail_ref[slot] = y_off * jnp.exp2(a_pair) + dch_ref[:, ch] * xp
            total = a_pair[chunk - 1:chunk, :]
            return (xdt * jnp.exp2(total - a_pair)).astype(BF16), jnp.exp2(total)

        def unit_back(g, q):
            slot = (g * pairs + q) % 2
            ch = slice(g * gw + q * LANES, g * gw + (q + 1) * LANES)
            y = pair_ref[slot] + tail_ref[slot]
            yg_ref[g % 2, :, q * LANES:(q + 1) * LANES] = y * sz_ref[rows, ch].astype(F32)

        def group_back(g):
            y_g = yg_ref[g % 2]
            ms = jnp.mean(y_g * y_g, axis=-1, keepdims=True)
            gch = slice(g * gw, (g + 1) * gw)
            yn_ref[rows, gch] = (y_g * lax.rsqrt(ms + EPS) * ng_ref[:, gch]).astype(BF16)

        group_front(0)
        backs = []
        for g in range(n_groups):
            if g + 1 < n_groups:
                group_front(g + 1)
            xstate, decay_end = [], []
            for q in range(pairs):
                next(pieces, lambda: None)()
                xs_q, de_q = unit_front(g, q)
                xstate.append(xs_q)
                decay_end.append(de_q)
                for back in backs:
                    back()
                backs = [functools.partial(unit_back, g, q)]
                if q == pairs - 1:
                    backs.append(functools.partial(group_back, g))
            xstate = jnp.concatenate(xstate, axis=1)
            decay_end = jnp.concatenate(decay_end, axis=1)
            state_ref[g] = state_ref[g] * decay_end + _dot_tn(b_of(g), xstate)
        for back in backs:
            back()
        for piece in pieces:
            piece()


def _mixer(h, proj, a_log_row, d_skip, norm_g, w_out_a, w_out_ssm, w_mix_out, seq):
    ya, sz, xs, ga, gb, dt = proj
    n_tok, d = h.shape
    d_inner = sz.shape[1]
    t = MIX_TOKENS
    n_tiles = n_tok // t
    ssd_row = lambda i: (jnp.minimum(i, n_tiles - 1), 0)
    proj_row = lambda i: (jnp.maximum(i - 1, 0), 0)
    vec_args = [a_log_row,
                jnp.repeat(d_skip.astype(F32), SSM_HEAD_DIM).reshape(1, d_inner),
                norm_g.reshape(1, d_inner).astype(F32)]
    w_args = [w_out_a.astype(BF16), w_out_ssm.astype(BF16), w_mix_out.astype(BF16)]
    proj_args = [ya, ga, gb, h]
    ssd_args = [sz, xs, dt]
    in_specs = ([pl.BlockSpec((t, a.shape[1]), proj_row) for a in proj_args]
                + [pl.BlockSpec((t, a.shape[1]), ssd_row) for a in ssd_args]
                + [_resident(a.shape) for a in vec_args + w_args])
    scratch = [
        pltpu.VMEM((SSM_GROUPS, SSM_STATE, d_inner // SSM_GROUPS), F32),
        pltpu.VMEM((t, d_inner), BF16),
        pltpu.VMEM((2, SSM_CHUNK, SSM_CHUNK + d_inner // SSM_GROUPS), F32),
        pltpu.VMEM((2, SSM_CHUNK, LANES), F32),
        pltpu.VMEM((2, SSM_CHUNK, LANES), F32),
        pltpu.VMEM((2, SSM_CHUNK, d_inner // SSM_GROUPS), F32),
        pltpu.VMEM((d // MXU_DIM, SSM_CHUNK, MXU_DIM), F32),
        pltpu.VMEM((d // MXU_DIM, SSM_CHUNK, MXU_DIM), F32),
        pltpu.VMEM((SSM_CHUNK, d), BF16),
    ]
    return pl.pallas_call(
        functools.partial(_mixer_body, d_inner=d_inner, n_groups=SSM_GROUPS,
                          n_state=SSM_STATE, tiles_per_seq=seq // t),
        grid=(n_tiles + 1,),
        in_specs=in_specs,
        out_specs=pl.BlockSpec((t, d), proj_row),
        out_shape=jax.ShapeDtypeStruct((n_tok, d), F32),
        scratch_shapes=scratch,
        compiler_params=_params("arbitrary"),
        name="mixer",
    )(*proj_args, *ssd_args, *vec_args, *w_args)


def _kv_body(m_ref, g_ref, w_ref, o_ref):
    u = _rms(m_ref[...], g_ref[...]).astype(BF16)
    o_ref[...] = _dot(u, w_ref[...]).astype(BF16)


def _kv_proj(mem2d, norm_g, w_kv):
    n_tok, d = mem2d.shape
    tm = KV_TOKENS
    return pl.pallas_call(
        _kv_body,
        grid=(n_tok // tm,),
        in_specs=[pl.BlockSpec((tm, d), lambda i: (i, 0)), _resident((1, d)),
                  _resident(w_kv.shape)],
        out_specs=pl.BlockSpec((tm, w_kv.shape[1]), lambda i: (i, 0)),
        out_shape=jax.ShapeDtypeStruct((n_tok, w_kv.shape[1]), BF16),
        compiler_params=_params("parallel"),
        name="kv_proj",
    )(mem2d, norm_g.reshape(1, d), w_kv.astype(BF16))


def _xattn_body(hp_ref, h_ref, g_ref, wq_ref, k_ref, v_ref, wo_ref, o_ref,
                att_ref, q_ref, s_ref, pv_ref, *, heads):
    d = h_ref.shape[1]
    dh = d // heads
    scale = 1.0 / math.sqrt(dh)
    ch = [slice(j * dh, (j + 1) * dh) for j in range(heads)]

    @pl.when(pl.program_id(0) == 0)
    def _():
        att_ref[...] = jnp.zeros_like(att_ref)

    att_prev = att_ref[...]
    u = _rms(h_ref[...], g_ref[...]).astype(BF16)

    def out_piece(j):
        o_ref[:, ch[j]] = hp_ref[:, ch[j]] + _dot(att_prev, wo_ref[:, ch[j]])

    def q_proj(j):
        q_ref[j % 2] = _dot(u, wq_ref[:, ch[j]])

    def scores(j):
        s_ref[j % 2] = _dot_nt(q_ref[j % 2].astype(BF16), k_ref[0, :, ch[j]])

    def attend(j):
        s = s_ref[j % 2] * scale
        p = jnp.exp(s - jnp.max(s, axis=-1, keepdims=True))
        p = p / jnp.sum(p, axis=-1, keepdims=True)
        pv_ref[j % 2] = _dot(p.astype(BF16), v_ref[0, :, ch[j]])

    def collect(j):
        att_ref[:, ch[j]] = pv_ref[j % 2].astype(BF16)

    out_piece(0)
    q_proj(0)
    scores(0)
    for j in range(heads):
        if j + 1 < heads:
            out_piece(j + 1)
            q_proj(j + 1)
            scores(j + 1)
        attend(j)
        if j > 0:
            collect(j - 1)
    collect(heads - 1)


def _xattn(h, kv, norm_g, w_q, w_o, batch, seq):
    n_tok, d = h.shape
    m = kv.shape[0] // batch
    kv3 = kv.reshape(batch, m, 2 * d)
    tm = ATTN_TOKENS
    tiles = seq // tm
    n_tiles = n_tok // tm
    dh = d // XATTN_HEADS
    cur = lambda i: jnp.minimum(i, n_tiles - 1)
    prev_row = lambda i: (jnp.maximum(i - 1, 0), 0)
    return pl.pallas_call(
        functools.partial(_xattn_body, heads=XATTN_HEADS),
        grid=(n_tiles + 1,),
        in_specs=[pl.BlockSpec((tm, d), prev_row),
                  pl.BlockSpec((tm, d), lambda i: (cur(i), 0)),
                  _resident((1, d)), _resident((d, d)),
                  pl.BlockSpec((1, m, d), lambda i: (cur(i) // tiles, 0, 0)),
                  pl.BlockSpec((1, m, d), lambda i: (cur(i) // tiles, 0, 1)),
                  _resident((d, d))],
        out_specs=pl.BlockSpec((tm, d), prev_row),
        out_shape=jax.ShapeDtypeStruct((n_tok, d), F32),
        scratch_shapes=[pltpu.VMEM((tm, d), BF16),
                        pltpu.VMEM((2, tm, dh), F32),
                        pltpu.VMEM((2, tm, m), F32),
                        pltpu.VMEM((2, tm, dh), F32)],
        compiler_params=_params("arbitrary"),
        name="xattn",
    )(h, h, norm_g.reshape(1, d), w_q.astype(BF16), kv3, kv3, w_o.astype(BF16))


def _split_w_in(w_in, d, d_inner, conv_ch, n_heads):
    n_front = 3 * d + d_inner + conv_ch
    w_front = w_in[:, :n_front].astype(BF16)
    w_gates = w_in[:, n_front + n_heads:].astype(BF16)
    w_dt = jnp.pad(w_in[:, n_front:n_front + n_heads], ((0, 0), (0, LANES - n_heads)))
    return w_front, w_gates, w_dt.astype(BF16)


def kernel(x, mem, ffn1_norm, ffn1_w_gate_up, ffn1_w_down, mix_norm, w_in, conv_a_w, w_out_a, ssm_conv_w, ssm_conv_b, ssm_dt_bias, ssm_a_log, ssm_d, ssm_norm, w_out_ssm, w_mix_out, xattn_norm, mem_norm, w_q, w_kv, w_o_x, ffn2_norm, ffn2_w_gate_up, ffn2_w_down, final_norm):
    batch, seq, d = x.shape
    depth = w_in.shape[0]
    d_inner = w_out_ssm.shape[1]
    conv_ch = ssm_conv_w.shape[2]
    n_heads = ssm_d.shape[1]
    h = x.reshape(batch * seq, d)
    mem2d = mem.reshape(-1, d)

    def head_row(v):
        return jnp.pad(v.astype(F32), (0, LANES - n_heads)).reshape(1, LANES)

    for i in range(depth):
        last = i == depth - 1
        h = _ffn(h, ffn1_norm[i], ffn1_w_gate_up[i], ffn1_w_down[i])
        w_front, w_gates, w_dt = _split_w_in(w_in[i], d, d_inner, conv_ch, n_heads)
        proj = _inproj(h, mix_norm[i], w_front, w_gates, w_dt, conv_a_w[i], ssm_conv_w[i],
                       ssm_conv_b[i], head_row(ssm_dt_bias[i]), d_inner, seq)
        h = _mixer(h, proj, head_row(ssm_a_log[i]), ssm_d[i], ssm_norm[i], w_out_a[i],
                   w_out_ssm[i], w_mix_out[i], seq)
        kv = _kv_proj(mem2d, mem_norm[i], w_kv[i])
        h = _xattn(h, kv, xattn_norm[i], w_q[i], w_o_x[i], batch, seq)
        h = _ffn(h, ffn2_norm[i], ffn2_w_gate_up[i], ffn2_w_down[i],
                 final_g=final_norm if last else None)
    return h.reshape(batch, seq, d)
```

```python
import functools
import math

import jax
import jax.numpy as jnp
from jax import lax
from jax.experimental import pallas as pl
from jax.experimental.pallas import tpu as pltpu

F32 = jnp.float32
BF16 = jnp.bfloat16

EPS = 1e-6
FFN_RES_WEIGHT = 0.5
SSM_HEAD_DIM = 64
SSM_GROUPS = 4
SSM_STATE = 128
SSM_CHUNK = 128
XATTN_HEADS = 4
LOG2E = math.log2(math.e)

LANES = 128
SUBLANES = 8
MXU_DIM = 256
VMEM_LIMIT_BYTES = 60 * 1024 * 1024

FFN_TOKENS = 1024
PROJ_TOKENS = 512
PROJ_COLS = 512
MIX_TOKENS = 512
ATTN_TOKENS = 1024
KV_TOKENS = 256


def _dot(a, b):
    return jnp.dot(a, b, preferred_element_type=F32)


def _dot_nt(a, b):
    return lax.dot_general(a, b, (((1,), (1,)), ((), ())), preferred_element_type=F32)


def _dot_tn(a, b):
    return lax.dot_general(a, b, (((0,), (0,)), ((), ())), preferred_element_type=F32)


def _rms(x, g):
    ms = jnp.mean(x * x, axis=-1, keepdims=True)
    return x * lax.rsqrt(ms + EPS) * g


def _silu(x):
    return x * jax.nn.sigmoid(x)


def _resident(shape):
    zeros = (0,) * len(shape)
    return pl.BlockSpec(shape, lambda *_: zeros, pipeline_mode=pl.Buffered(1))


def _params(*semantics, flags=None):
    return pltpu.CompilerParams(dimension_semantics=semantics,
                                vmem_limit_bytes=VMEM_LIMIT_BYTES, flags=flags)


def _ff_chunks(d_ff):
    chunks, start = [], 0
    while start < d_ff:
        n = min(4 * MXU_DIM, d_ff - start)
        chunks.append((start, n))
        start += n
    return tuple(chunks)


def _ffn_body(x_ref, g_ref, wgu_ref, wd_ref, *rest, chunks, d_ff, final):
    if final:
        fg_ref, o_ref = rest
    else:
        (o_ref,) = rest
    x = x_ref[...]
    u = _rms(x, g_ref[...]).astype(BF16)
    acc = None
    for start, n in chunks:
        gate = _dot(u, wgu_ref[:, start:start + n])
        up = _dot(u, wgu_ref[:, d_ff + start:d_ff + start + n])
        act = (_silu(gate) * up).astype(BF16)
        part = _dot(act, wd_ref[start:start + n, :])
        acc = part if acc is None else acc + part
    h = x + FFN_RES_WEIGHT * acc
    if final:
        h = _rms(h, fg_ref[...])
    o_ref[...] = h


def _ffn(x, norm_g, w_gate_up, w_down, final_g=None):
    n_tok, d = x.shape
    d_ff = w_down.shape[0]
    tm = FFN_TOKENS
    tile = pl.BlockSpec((tm, d), lambda i: (i, 0))
    in_specs = [tile, _resident((1, d)), _resident((d, 2 * d_ff)), _resident((d_ff, d))]
    args = [x, norm_g.reshape(1, d), w_gate_up.astype(BF16), w_down.astype(BF16)]
    if final_g is not None:
        in_specs.append(_resident((1, d)))
        args.append(final_g.reshape(1, d))
    return pl.pallas_call(
        functools.partial(_ffn_body, chunks=_ff_chunks(d_ff), d_ff=d_ff,
                          final=final_g is not None),
        grid=(n_tok // tm,),
        in_specs=in_specs,
        out_specs=tile,
        out_shape=jax.ShapeDtypeStruct((n_tok, d), F32),
        compiler_params=_params("parallel"),
        name="ffn_final" if final_g is not None else "ffn",
    )(*args)


RING_SLOTS = 3


def _conv_rows(ring_ref, slot, tm, taps):
    k_taps = taps.shape[0]
    acc = None
    for back in range(k_taps):
        shifted = ring_ref[slot, SUBLANES - back:SUBLANES - back + tm, :]
        term = shifted * taps[k_taps - 1 - back:k_taps - back, :]
        acc = term if acc is None else acc + term
    return acc


def _inproj_body(h_ref, g_ref, wf_ref, wg_ref, wdt_ref, caw_ref, csw_ref, csb_ref, dtb_ref,
                 ya_ref, sz_ref, xs_ref, ga_ref, gb_ref, dt_ref,
                 ring_ref, side_ref, halo_c_ref, halo_x_ref, *, tiles_per_seq):
    tm, d = h_ref.shape
    d_inner = sz_ref.shape[1]
    conv_ch = xs_ref.shape[1]
    cw = PROJ_COLS

    @pl.when(pl.program_id(0) % tiles_per_seq == 0)
    def _():
        halo_c_ref[...] = jnp.zeros_like(halo_c_ref)
        halo_x_ref[...] = jnp.zeros_like(halo_x_ref)

    u = _rms(h_ref[...], g_ref[...]).astype(BF16)
    n_front = wf_ref.shape[1]

    def proj(off):
        if off < n_front:
            return _dot(u, wf_ref[:, off:off + cw])
        return _dot(u, wg_ref[:, off - n_front:off - n_front + cw])

    def park(slot, x, halo_ref=None, cols=None):
        if halo_ref is not None:
            ring_ref[slot, 0:SUBLANES, :] = halo_ref[:, cols]
            halo_ref[:, cols] = x[tm - SUBLANES:, :]
        ring_ref[slot, SUBLANES:SUBLANES + tm, :] = x

    z_off, x_off, g_off = 3 * d, 3 * d + d_inner, 3 * d + d_inner + conv_ch

    def conv3_job(s):
        cols = slice(s, s + cw)

        def front(slot):
            park(slot, proj(d + s) * proj(2 * d + s), halo_c_ref, cols)
            side_ref[slot] = proj(s)

        def back(slot):
            conv = _conv_rows(ring_ref, slot, tm, caw_ref[:, cols])
            ya_ref[:, cols] = (side_ref[slot] * conv).astype(BF16)
        return front, back

    def conv4_job(s):
        cols = slice(s, s + cw)

        def front(slot):
            park(slot, proj(x_off + s), halo_x_ref, cols)

        def back(slot):
            conv = _conv_rows(ring_ref, slot, tm, csw_ref[:, cols])
            xs_ref[:, cols] = _silu(conv + csb_ref[:, cols]).astype(BF16)
        return front, back

    def gate_job(o_ref, off, s, fn):
        def front(slot):
            park(slot, proj(off + s))

        def back(slot):
            o_ref[:, s:s + cw] = fn(ring_ref[slot, SUBLANES:SUBLANES + tm, :]).astype(BF16)
        return front, back

    heavy = [conv4_job(s) for s in range(0, conv_ch, cw)]
    light = ([gate_job(sz_ref, z_off, s, _silu) for s in range(0, d_inner, cw)]
             + [conv3_job(s) for s in range(0, d, cw)]
             + [gate_job(ga_ref, g_off, s, jax.nn.sigmoid) for s in range(0, d, cw)]
             + [gate_job(gb_ref, g_off + d, s, jax.nn.sigmoid) for s in range(0, d, cw)])
    jobs = []
    while heavy or light:
        if heavy:
            jobs.append(heavy.pop(0))
        if light:
            jobs.append(light.pop(0))
    pending = None
    for j, (front, back) in enumerate(jobs):
        front(j % RING_SLOTS)
        if pending is not None:
            pending()
        pending = functools.partial(back, j % RING_SLOTS)
    dt_ref[...] = jax.nn.softplus(_dot(u, wdt_ref[...]) + dtb_ref[...])
    pending()


def _inproj(h, norm_g, w_front, w_gates, w_dt, conv_a_w, ssm_conv_w, ssm_conv_b, dt_bias_row,
            d_inner, seq):
    n_tok, d = h.shape
    conv_ch = ssm_conv_w.shape[1]
    tm = PROJ_TOKENS
    row = lambda i: (i, 0)
    widths = (d, d_inner, conv_ch, d, d)
    out_shape = [jax.ShapeDtypeStruct((n_tok, n), BF16) for n in widths]
    out_shape.append(jax.ShapeDtypeStruct((n_tok, LANES), F32))
    out_specs = [pl.BlockSpec((tm, n), row) for n in widths]
    out_specs.append(pl.BlockSpec((tm, LANES), row))
    vec_args = [norm_g.reshape(1, d), w_front, w_gates, w_dt, conv_a_w.astype(F32),
                ssm_conv_w.astype(F32), ssm_conv_b.reshape(1, conv_ch).astype(F32),
                dt_bias_row]
    return pl.pallas_call(
        functools.partial(_inproj_body, tiles_per_seq=seq // tm),
        grid=(n_tok // tm,),
        in_specs=[pl.BlockSpec((tm, d), row)] + [_resident(a.shape) for a in vec_args],
        out_specs=out_specs,
        out_shape=out_shape,
        scratch_shapes=[pltpu.VMEM((RING_SLOTS, tm + SUBLANES, PROJ_COLS), F32),
                        pltpu.VMEM((RING_SLOTS, tm, PROJ_COLS), F32),
                        pltpu.VMEM((SUBLANES, d), F32),
                        pltpu.VMEM((SUBLANES, conv_ch), F32)],
        compiler_params=_params("arbitrary"),
        name="inproj",
    )(h, *vec_args)


def _split3(v):
    hi = v.astype(BF16)
    r1 = v - hi.astype(F32)
    mid = r1.astype(BF16)
    lo = (r1 - mid.astype(F32)).astype(BF16)
    return jnp.concatenate([hi, mid, lo], axis=1)


def _sum3(v):
    n = v.shape[1] // 3
    return v[:, :n] + v[:, n:2 * n] + v[:, 2 * n:]


def _lane_bcast(m, col, width):
    return jnp.broadcast_to(m[:, col:col + 1], (m.shape[0], width))


def _proj_pieces(rows, yn_ref, ya_ref, ga_ref, gb_ref, h_ref, woa_ref, wos_ref, wmx_ref,
                 o_ref, pyb_ref, pya_ref, mg_ref):
    d = h_ref.shape[1]
    n_tiles = d // MXU_DIM
    cols = [slice(n * MXU_DIM, (n + 1) * MXU_DIM) for n in range(n_tiles)]

    def merge(n):
        mg_ref[:, cols[n]] = (ga_ref[rows, cols[n]].astype(F32) * pya_ref[n]
                              + gb_ref[rows, cols[n]].astype(F32) * pyb_ref[n]).astype(BF16)

    def yb_piece(n):
        def run():
            pyb_ref[n] = _dot(yn_ref[rows, :], wos_ref[:, cols[n]])
        return run

    def ya_piece(n):
        def run():
            pya_ref[n] = _dot(ya_ref[rows, :], woa_ref[:, cols[n]])
            if n > 0:
                merge(n - 1)
        return run

    def mix_piece(n):
        def run():
            if n == 0:
                merge(n_tiles - 1)
            o_ref[rows, cols[n]] = h_ref[rows, cols[n]] + _dot(mg_ref[...], wmx_ref[:, cols[n]])
        return run

    return ([yb_piece(n) for n in range(n_tiles)] + [ya_piece(n) for n in range(n_tiles)]
            + [mix_piece(n) for n in range(n_tiles)])


def _mixer_body(ya_ref, ga_ref, gb_ref, h_ref, sz_ref, xs_ref, dt_ref,
                alog_ref, dch_ref, ng_ref, woa_ref, wos_ref, wmx_ref, o_ref,
                state_ref, yn_ref, grp_ref, pair_ref, tail_ref, yg_ref,
                pyb_ref, pya_ref, mg_ref, *, d_inner, n_groups, n_state, tiles_per_seq):
    t_tile = h_ref.shape[0]
    chunk = SSM_CHUNK
    gw = d_inner // n_groups
    pairs = gw // LANES
    hpg = gw // SSM_HEAD_DIM
    b_off, c_off = d_inner, d_inner + n_groups * n_state
    assert h_ref.shape[1] // MXU_DIM <= pairs
    step = pl.program_id(0)

    @pl.when(step == 0)
    def _():
        yn_ref[...] = jnp.zeros_like(yn_ref)

    @pl.when(step % tiles_per_seq == 0)
    def _():
        state_ref[...] = jnp.zeros_like(state_ref)

    a_log2 = -jnp.exp(alog_ref[...]) * LOG2E

    row_i = lax.broadcasted_iota(jnp.int32, (chunk, chunk), 0)
    col_i = lax.broadcasted_iota(jnp.int32, (chunk, chunk), 1)
    causal = row_i >= col_i
    tril = causal.astype(BF16)
    low_half = col_i < SSM_HEAD_DIM

    for c in range(t_tile // chunk):
        rows = slice(c * chunk, (c + 1) * chunk)
        pieces = iter(_proj_pieces(rows, yn_ref, ya_ref, ga_ref, gb_ref, h_ref, woa_ref,
                                   wos_ref, wmx_ref, o_ref, pyb_ref, pya_ref, mg_ref))

        dt_c = dt_ref[rows, :]
        acs = _sum3(_dot(tril, _split3(dt_c * a_log2)))
        acs_t = acs.T

        def b_of(g):
            return xs_ref[rows, b_off + g * n_state:b_off + (g + 1) * n_state]

        def group_front(g):
            cm = xs_ref[rows, c_off + g * n_state:c_off + (g + 1) * n_state]
            grp_ref[g % 2, :, 0:chunk] = _dot_nt(cm, b_of(g))
            grp_ref[g % 2, :, chunk:] = _dot(cm, state_ref[g].astype(BF16))

        def unit_front(g, q):
            slot = (g * pairs + q) % 2
            h0 = g * hpg + 2 * q
            ch = slice(g * gw + q * LANES, g * gw + (q + 1) * LANES)
            cb = grp_ref[g % 2, :, 0:chunk]
            a0 = _lane_bcast(acs, h0, chunk)
            a1 = _lane_bcast(acs, h0 + 1, chunk)
            m0 = cb * jnp.where(causal, jnp.exp2(a0 - acs_t[h0:h0 + 1, :]), 0.0)
            m1 = cb * jnp.where(causal, jnp.exp2(a1 - acs_t[h0 + 1:h0 + 2, :]), 0.0)
            a_pair = jnp.where(low_half, a0, a1)
            xp = xs_ref[rows, ch].astype(F32)
            xdt = xp * jnp.where(low_half, _lane_bcast(dt_c, h0, LANES),
                                 _lane_bcast(dt_c, h0 + 1, LANES))
            m01 = jnp.concatenate([m0.astype(BF16), m1.astype(BF16)], axis=1)
            x01 = jnp.concatenate([jnp.where(low_half, xdt, 0.0).astype(BF16),
                                   jnp.where(low_half, 0.0, xdt).astype(BF16)], axis=0)
            pair_ref[slot] = _dot(m01, x01)
            y_off = grp_ref[g % 2, :, chunk + q * LANES:chunk + (q + 1) * LANES]
            tail_ref[slot] = y_off * jnp.exp2(a_pair) + dch_ref[:, ch] * xp
            total = a_pair[chunk - 1:chunk, :]
            return (xdt * jnp.exp2(total - a_pair)).astype(BF16), jnp.exp2(total)

        def unit_back(g, q):
            slot = (g * pairs + q) % 2
            ch = slice(g * gw + q * LANES, g * gw + (q + 1) * LANES)
            y = pair_ref[slot] + tail_ref[slot]
            yg_ref[g % 2, :, q * LANES:(q + 1) * LANES] = y * sz_ref[rows, ch].astype(F32)

        def group_back(g):
            y_g = yg_ref[g % 2]
            ms = jnp.mean(y_g * y_g, axis=-1, keepdims=True)
            gch = slice(g * gw, (g + 1) * gw)
            yn_ref[rows, gch] = (y_g * lax.rsqrt(ms + EPS) * ng_ref[:, gch]).astype(BF16)

        group_front(0)
        backs = []
        for g in range(n_groups):
            if g + 1 < n_groups:
                group_front(g + 1)
            xstate, decay_end = [], []
            for q in range(pairs):
                next(pieces, lambda: None)()
                xs_q, de_q = unit_front(g, q)
                xstate.append(xs_q)
                decay_end.append(de_q)
                for back in backs:
                    back()
                backs = [functools.partial(unit_back, g, q)]
                if q == pairs - 1:
                    backs.append(functools.partial(group_back, g))
            xstate = jnp.concatenate(xstate, axis=1)
            decay_end = jnp.concatenate(decay_end, axis=1)
            state_ref[g] = state_ref[g] * decay_end + _dot_tn(b_of(g), xstate)
        for back in backs:
            back()
        for piece in pieces:
            piece()


def _mixer(h, proj, a_log_row, d_skip, norm_g, w_out_a, w_out_ssm, w_mix_out, seq):
    ya, sz, xs, ga, gb, dt = proj
    n_tok, d = h.shape
    d_inner = sz.shape[1]
    t = MIX_TOKENS
    n_tiles = n_tok // t
    ssd_row = lambda i: (jnp.minimum(i, n_tiles - 1), 0)
    proj_row = lambda i: (jnp.maximum(i - 1, 0), 0)
    vec_args = [a_log_row,
                jnp.repeat(d_skip.astype(F32), SSM_HEAD_DIM).reshape(1, d_inner),
                norm_g.reshape(1, d_inner).astype(F32)]
    w_args = [w_out_a.astype(BF16), w_out_ssm.astype(BF16), w_mix_out.astype(BF16)]
    proj_args = [ya, ga, gb, h]
    ssd_args = [sz, xs, dt]
    in_specs = ([pl.BlockSpec((t, a.shape[1]), proj_row) for a in proj_args]
                + [pl.BlockSpec((t, a.shape[1]), ssd_row) for a in ssd_args]
                + [_resident(a.shape) for a in vec_args + w_args])
    scratch = [
        pltpu.VMEM((SSM_GROUPS, SSM_STATE, d_inner // SSM_GROUPS), F32),
        pltpu.VMEM((t, d_inner), BF16),
        pltpu.VMEM((2, SSM_CHUNK, SSM_CHUNK + d_inner // SSM_GROUPS), F32),
        pltpu.VMEM((2, SSM_CHUNK, LANES), F32),
        pltpu.VMEM((2, SSM_CHUNK, LANES), F32),
        pltpu.VMEM((2, SSM_CHUNK, d_inner // SSM_GROUPS), F32),
        pltpu.VMEM((d // MXU_DIM, SSM_CHUNK, MXU_DIM), F32),
        pltpu.VMEM((d // MXU_DIM, SSM_CHUNK, MXU_DIM), F32),
        pltpu.VMEM((SSM_CHUNK, d), BF16),
    ]
    return pl.pallas_call(
        functools.partial(_mixer_body, d_inner=d_inner, n_groups=SSM_GROUPS,
                          n_state=SSM_STATE, tiles_per_seq=seq // t),
        grid=(n_tiles + 1,),
        in_specs=in_specs,
        out_specs=pl.BlockSpec((t, d), proj_row),
        out_shape=jax.ShapeDtypeStruct((n_tok, d), F32),
        scratch_shapes=scratch,
        compiler_params=_params("arbitrary"),
        name="mixer",
    )(*proj_args, *ssd_args, *vec_args, *w_args)


def _kv_body(m_ref, g_ref, w_ref, o_ref):
    u = _rms(m_ref[...], g_ref[...]).astype(BF16)
    o_ref[...] = _dot(u, w_ref[...]).astype(BF16)


def _kv_proj(mem2d, norm_g, w_kv):
    n_tok, d = mem2d.shape
    tm = KV_TOKENS
    return pl.pallas_call(
        _kv_body,
        grid=(n_tok // tm,),
        in_specs=[pl.BlockSpec((tm, d), lambda i: (i, 0)), _resident((1, d)),
                  _resident(w_kv.shape)],
        out_specs=pl.BlockSpec((tm, w_kv.shape[1]), lambda i: (i, 0)),
        out_shape=jax.ShapeDtypeStruct((n_tok, w_kv.shape[1]), BF16),
        compiler_params=_params("parallel"),
        name="kv_proj",
    )(mem2d, norm_g.reshape(1, d), w_kv.astype(BF16))


def _xattn_body(hp_ref, h_ref, g_ref, wq_ref, k_ref, v_ref, wo_ref, o_ref,
                att_ref, q_ref, s_ref, pv_ref, *, heads):
    d = h_ref.shape[1]
    dh = d // heads
    scale = 1.0 / math.sqrt(dh)
    ch = [slice(j * dh, (j + 1) * dh) for j in range(heads)]

    @pl.when(pl.program_id(0) == 0)
    def _():
        att_ref[...] = jnp.zeros_like(att_ref)

    att_prev = att_ref[...]
    u = _rms(h_ref[...], g_ref[...]).astype(BF16)

    def out_piece(j):
        o_ref[:, ch[j]] = hp_ref[:, ch[j]] + _dot(att_prev, wo_ref[:, ch[j]])

    def q_proj(j):
        q_ref[j % 2] = _dot(u, wq_ref[:, ch[j]])

    def scores(j):
        s_ref[j % 2] = _dot_nt(q_ref[j % 2].astype(BF16), k_ref[0, :, ch[j]])

    def attend(j):
        s = s_ref[j % 2] * scale
        p = jnp.exp(s - jnp.max(s, axis=-1, keepdims=True))
        p = p / jnp.sum(p, axis=-1, keepdims=True)
        pv_ref[j % 2] = _dot(p.astype(BF16), v_ref[0, :, ch[j]])

    def collect(j):
        att_ref[:, ch[j]] = pv_ref[j % 2].astype(BF16)

    out_piece(0)
    q_proj(0)
    scores(0)
    for j in range(heads):
        if j + 1 < heads:
            out_piece(j + 1)
            q_proj(j + 1)
            scores(j + 1)
        attend(j)
        if j > 0:
            collect(j - 1)
    collect(heads - 1)


def _xattn(h, kv, norm_g, w_q, w_o, batch, seq):
    n_tok, d = h.shape
    m = kv.shape[0] // batch
    kv3 = kv.reshape(batch, m, 2 * d)
    tm = ATTN_TOKENS
    tiles = seq // tm
    n_tiles = n_tok // tm
    dh = d // XATTN_HEADS
    cur = lambda i: jnp.minimum(i, n_tiles - 1)
    prev_row = lambda i: (jnp.maximum(i - 1, 0), 0)
    return pl.pallas_call(
        functools.partial(_xattn_body, heads=XATTN_HEADS),
        grid=(n_tiles + 1,),
        in_specs=[pl.BlockSpec((tm, d), prev_row),
                  pl.BlockSpec((tm, d), lambda i: (cur(i), 0)),
                  _resident((1, d)), _resident((d, d)),
                  pl.BlockSpec((1, m, d), lambda i: (cur(i) // tiles, 0, 0)),
                  pl.BlockSpec((1, m, d), lambda i: (cur(i) // tiles, 0, 1)),
                  _resident((d, d))],
        out_specs=pl.BlockSpec((tm, d), prev_row),
        out_shape=jax.ShapeDtypeStruct((n_tok, d), F32),
        scratch_shapes=[pltpu.VMEM((tm, d), BF16),
                        pltpu.VMEM((2, tm, dh), F32),
                        pltpu.VMEM((2, tm, m), F32),
                        pltpu.VMEM((2, tm, dh), F32)],
        compiler_params=_params("arbitrary"),
        name="xattn",
    )(h, h, norm_g.reshape(1, d), w_q.astype(BF16), kv3, kv3, w_o.astype(BF16))


def _split_w_in(w_in, d, d_inner, conv_ch, n_heads):
    n_front = 3 * d + d_inner + conv_ch
    w_front = w_in[:, :n_front].astype(BF16)
    w_gates = w_in[:, n_front + n_heads:].astype(BF16)
    w_dt = jnp.pad(w_in[:, n_front:n_front + n_heads], ((0, 0), (0, LANES - n_heads)))
    return w_front, w_gates, w_dt.astype(BF16)


def kernel(x, mem, ffn1_norm, ffn1_w_gate_up, ffn1_w_down, mix_norm, w_in, conv_a_w, w_out_a, ssm_conv_w, ssm_conv_b, ssm_dt_bias, ssm_a_log, ssm_d, ssm_norm, w_out_ssm, w_mix_out, xattn_norm, mem_norm, w_q, w_kv, w_o_x, ffn2_norm, ffn2_w_gate_up, ffn2_w_down, final_norm):
    batch, seq, d = x.shape
    depth = w_in.shape[0]
    d_inner = w_out_ssm.shape[1]
    conv_ch = ssm_conv_w.shape[2]
    n_heads = ssm_d.shape[1]
    h = x.reshape(batch * seq, d)
    mem2d = mem.reshape(-1, d)

    def head_row(v):
        return jnp.pad(v.astype(F32), (0, LANES - n_heads)).reshape(1, LANES)

    for i in range(depth):
        last = i == depth - 1
        h = _ffn(h, ffn1_norm[i], ffn1_w_gate_up[i], ffn1_w_down[i])
        w_front, w_gates, w_dt = _split_w_in(w_in[i], d, d_inner, conv_ch, n_heads)
        proj = _inproj(h, mix_norm[i], w_front, w_gates, w_dt, conv_a_w[i], ssm_conv_w[i],
                       ssm_conv_b[i], head_row(ssm_dt_bias[i]), d_inner, seq)
        h = _mixer(h, proj, head_row(ssm_a_log[i]), ssm_d[i], ssm_norm[i], w_out_a[i],
                   w_out_ssm[i], w_mix_out[i], seq)
        kv = _kv_proj(mem2d, mem_norm[i], w_kv[i])
        h = _xattn(h, kv, xattn_norm[i], w_q[i], w_o_x[i], batch, seq)
        h = _ffn(h, ffn2_norm[i], ffn2_w_gate_up[i], ffn2_w_down[i],
                 final_g=final_norm if last else None)
    return h.reshape(batch, seq, d)
```

```python
import functools
import math

import jax
import jax.numpy as jnp
from jax import lax
from jax.experimental import pallas as pl
from jax.experimental.pallas import tpu as pltpu

F32 = jnp.float32
BF16 = jnp.bfloat16

EPS = 1e-6
FFN_RES_WEIGHT = 0.5
SSM_HEAD_DIM = 64
SSM_GROUPS = 4
SSM_STATE = 128
SSM_CHUNK = 128
XATTN_HEADS = 4
LOG2E = math.log2(math.e)

LANES = 128
SUBLANES = 8
MXU_DIM = 256
VMEM_LIMIT_BYTES = 60 * 1024 * 1024

FFN_TOKENS = 1024
PROJ_TOKENS = 512
PROJ_COLS = 512
MIX_TOKENS = 512
ATTN_TOKENS = 1024
KV_TOKENS = 256


def _dot(a, b):
    return jnp.dot(a, b, preferred_element_type=F32)


def _dot_nt(a, b):
    return lax.dot_general(a, b, (((1,), (1,)), ((), ())), preferred_element_type=F32)


def _dot_tn(a, b):
    return lax.dot_general(a, b, (((0,), (0,)), ((), ())), preferred_element_type=F32)


def _rms(x, g):
    ms = jnp.mean(x * x, axis=-1, keepdims=True)
    return x * lax.rsqrt(ms + EPS) * g


def _silu(x):
    return x * jax.nn.sigmoid(x)


def _resident(shape):
    zeros = (0,) * len(shape)
    return pl.BlockSpec(shape, lambda *_: zeros, pipeline_mode=pl.Buffered(1))


def _params(*semantics, flags=None):
    return pltpu.CompilerParams(dimension_semantics=semantics,
                                vmem_limit_bytes=VMEM_LIMIT_BYTES, flags=flags)


def _ff_chunks(d_ff):
    chunks, start = [], 0
    while start < d_ff:
        n = min(4 * MXU_DIM, d_ff - start)
        chunks.append((start, n))
        start += n
    return tuple(chunks)


def _ffn_body(x_ref, g_ref, wgu_ref, wd_ref, *rest, chunks, d_ff, final):
    if final:
        fg_ref, o_ref = rest
    else:
        (o_ref,) = rest
    x = x_ref[...]
    u = _rms(x, g_ref[...]).astype(BF16)
    acc = None
    for start, n in chunks:
        gate = _dot(u, wgu_ref[:, start:start + n])
        up = _dot(u, wgu_ref[:, d_ff + start:d_ff + start + n])
        act = (_silu(gate) * up).astype(BF16)
        part = _dot(act, wd_ref[start:start + n, :])
        acc = part if acc is None else acc + part
    h = x + FFN_RES_WEIGHT * acc
    if final:
        h = _rms(h, fg_ref[...])
    o_ref[...] = h


def _ffn(x, norm_g, w_gate_up, w_down, final_g=None):
    n_tok, d = x.shape
    d_ff = w_down.shape[0]
    tm = FFN_TOKENS
    tile = pl.BlockSpec((tm, d), lambda i: (i, 0))
    in_specs = [tile, _resident((1, d)), _resident((d, 2 * d_ff)), _resident((d_ff, d))]
    args = [x, norm_g.reshape(1, d), w_gate_up.astype(BF16), w_down.astype(BF16)]
    if final_g is not None:
        in_specs.append(_resident((1, d)))
        args.append(final_g.reshape(1, d))
    return pl.pallas_call(
        functools.partial(_ffn_body, chunks=_ff_chunks(d_ff), d_ff=d_ff,
                          final=final_g is not None),
        grid=(n_tok // tm,),
        in_specs=in_specs,
        out_specs=tile,
        out_shape=jax.ShapeDtypeStruct((n_tok, d), F32),
        compiler_params=_params("parallel"),
        name="ffn_final" if final_g is not None else "ffn",
    )(*args)


PROJ_CHUNKS = 2
RING_SLOTS = 3


def _conv_rows(ring_ref, slot, tm, taps):
    k_taps = taps.shape[0]
    acc = None
    for back in range(k_taps):
        shifted = ring_ref[slot, SUBLANES - back:SUBLANES - back + tm, :]
        term = shifted * taps[k_taps - 1 - back:k_taps - back, :]
        acc = term if acc is None else acc + term
    return acc


def _inproj_body(h_ref, g_ref, wf_ref, wg_ref, wdt_ref, caw_ref, csw_ref, csb_ref, dtb_ref,
                 ya_ref, sz_ref, xs_ref, ga_ref, gb_ref, dt_ref,
                 ring_ref, side_ref, halo_c_ref, halo_x_ref, *, tiles_per_seq):
    tm, d = h_ref.shape
    d_inner = sz_ref.shape[1]
    conv_ch = xs_ref.shape[1]
    cw = PROJ_COLS

    @pl.when(pl.program_id(0) % tiles_per_seq == 0)
    def _():
        halo_c_ref[...] = jnp.zeros_like(halo_c_ref)
        halo_x_ref[...] = jnp.zeros_like(halo_x_ref)

    u = _rms(h_ref[...], g_ref[...]).astype(BF16)
    n_front = wf_ref.shape[1]

    def proj(off):
        if off < n_front:
            return _dot(u, wf_ref[:, off:off + cw])
        return _dot(u, wg_ref[:, off - n_front:off - n_front + cw])

    def park(slot, x, halo_ref=None, cols=None):
        if halo_ref is not None:
            ring_ref[slot, 0:SUBLANES, :] = halo_ref[:, cols]
            halo_ref[:, cols] = x[tm - SUBLANES:, :]
        ring_ref[slot, SUBLANES:SUBLANES + tm, :] = x

    z_off, x_off, g_off = 3 * d, 3 * d + d_inner, 3 * d + d_inner + conv_ch

    def conv3_job(s):
        cols = slice(s, s + cw)

        def front(slot):
            park(slot, proj(d + s) * proj(2 * d + s), halo_c_ref, cols)
            side_ref[slot] = proj(s)

        def back(slot):
            conv = _conv_rows(ring_ref, slot, tm, caw_ref[:, cols])
            ya_ref[:, cols] = (side_ref[slot] * conv).astype(BF16)
        return front, back

    def conv4_job(s):
        cols = slice(s, s + cw)

        def front(slot):
            park(slot, proj(x_off + s), halo_x_ref, cols)

        def back(slot):
            conv = _conv_rows(ring_ref, slot, tm, csw_ref[:, cols])
            xs_ref[:, cols] = _silu(conv + csb_ref[:, cols]).astype(BF16)
        return front, back

    def gate_job(o_ref, off, s, fn):
        def front(slot):
            park(slot, proj(off + s))

        def back(slot):
            o_ref[:, s:s + cw] = fn(ring_ref[slot, SUBLANES:SUBLANES + tm, :]).astype(BF16)
        return front, back

    heavy = [conv4_job(s) for s in range(0, conv_ch, cw)]
    light = ([gate_job(sz_ref, z_off, s, _silu) for s in range(0, d_inner, cw)]
             + [conv3_job(s) for s in range(0, d, cw)]
             + [gate_job(ga_ref, g_off, s, jax.nn.sigmoid) for s in range(0, d, cw)]
             + [gate_job(gb_ref, g_off + d, s, jax.nn.sigmoid) for s in range(0, d, cw)])
    jobs = []
    while heavy or light:
        if heavy:
            jobs.append(heavy.pop(0))
        if light:
            jobs.append(light.pop(0))
    pending = None
    for j, (front, back) in enumerate(jobs):
        front(j % RING_SLOTS)
        if pending is not None:
            pending()
        pending = functools.partial(back, j % RING_SLOTS)
    dt_ref[...] = jax.nn.softplus(_dot(u, wdt_ref[...]) + dtb_ref[...])
    pending()


def _inproj(h, norm_g, w_front, w_gates, w_dt, conv_a_w, ssm_conv_w, ssm_conv_b, dt_bias_row,
            d_inner, seq):
    n_tok, d = h.shape
    conv_ch = ssm_conv_w.shape[1]
    tm = PROJ_TOKENS
    row = lambda i: (i, 0)
    widths = (d, d_inner, conv_ch, d, d)
    out_shape = [jax.ShapeDtypeStruct((n_tok, n), BF16) for n in widths]
    out_shape.append(jax.ShapeDtypeStruct((n_tok, LANES), F32))
    out_specs = [pl.BlockSpec((tm, n), row) for n in widths]
    out_specs.append(pl.BlockSpec((tm, LANES), row))
    vec_args = [norm_g.reshape(1, d), w_front, w_gates, w_dt, conv_a_w.astype(F32),
                ssm_conv_w.astype(F32), ssm_conv_b.reshape(1, conv_ch).astype(F32),
                dt_bias_row]
    return pl.pallas_call(
        functools.partial(_inproj_body, tiles_per_seq=seq // tm),
        grid=(n_tok // tm,),
        in_specs=[pl.BlockSpec((tm, d), row)] + [_resident(a.shape) for a in vec_args],
        out_specs=out_specs,
        out_shape=out_shape,
        scratch_shapes=[pltpu.VMEM((RING_SLOTS, tm + SUBLANES, PROJ_COLS), F32),
                        pltpu.VMEM((RING_SLOTS, tm, PROJ_COLS), F32),
                        pltpu.VMEM((SUBLANES, d), F32),
                        pltpu.VMEM((SUBLANES, conv_ch), F32)],
        compiler_params=_params("arbitrary"),
        name="inproj",
    )(h, *vec_args)


def _split3(v):
    hi = v.astype(BF16)
    r1 = v - hi.astype(F32)
    mid = r1.astype(BF16)
    lo = (r1 - mid.astype(F32)).astype(BF16)
    return jnp.concatenate([hi, mid, lo], axis=1)


def _sum3(v):
    n = v.shape[1] // 3
    return v[:, :n] + v[:, n:2 * n] + v[:, 2 * n:]


def _lane_bcast(m, col, width):
    return jnp.broadcast_to(m[:, col:col + 1], (m.shape[0], width))


def _proj_pieces(rows, yn_ref, ya_ref, ga_ref, gb_ref, h_ref, woa_ref, wos_ref, wmx_ref,
                 o_ref, pyb_ref, pya_ref, mg_ref):
    d = h_ref.shape[1]
    n_tiles = d // MXU_DIM
    cols = [slice(n * MXU_DIM, (n + 1) * MXU_DIM) for n in range(n_tiles)]

    def merge(n):
        mg_ref[:, cols[n]] = (ga_ref[rows, cols[n]].astype(F32) * pya_ref[n]
                              + gb_ref[rows, cols[n]].astype(F32) * pyb_ref[n]).astype(BF16)

    def yb_piece(n):
        def run():
            pyb_ref[n] = _dot(yn_ref[rows, :], wos_ref[:, cols[n]])
        return run

    def ya_piece(n):
        def run():
            pya_ref[n] = _dot(ya_ref[rows, :], woa_ref[:, cols[n]])
            if n > 0:
                merge(n - 1)
        return run

    def mix_piece(n):
        def run():
            if n == 0:
                merge(n_tiles - 1)
            o_ref[rows, cols[n]] = h_ref[rows, cols[n]] + _dot(mg_ref[...], wmx_ref[:, cols[n]])
        return run

    return ([yb_piece(n) for n in range(n_tiles)] + [ya_piece(n) for n in range(n_tiles)]
            + [mix_piece(n) for n in range(n_tiles)])


def _mixer_body(ya_ref, ga_ref, gb_ref, h_ref, sz_ref, xs_ref, dt_ref,
                alog_ref, dch_ref, ng_ref, woa_ref, wos_ref, wmx_ref, o_ref,
                state_ref, yn_ref, grp_ref, pair_ref, tail_ref, yg_ref,
                pyb_ref, pya_ref, mg_ref, *, d_inner, n_groups, n_state, tiles_per_seq):
    t_tile = h_ref.shape[0]
    chunk = SSM_CHUNK
    gw = d_inner // n_groups
    pairs = gw // LANES
    hpg = gw // SSM_HEAD_DIM
    b_off, c_off = d_inner, d_inner + n_groups * n_state
    assert h_ref.shape[1] // MXU_DIM <= pairs
    step = pl.program_id(0)

    @pl.when(step == 0)
    def _():
        yn_ref[...] = jnp.zeros_like(yn_ref)

    @pl.when(step % tiles_per_seq == 0)
    def _():
        state_ref[...] = jnp.zeros_like(state_ref)

    a_log2 = -jnp.exp(alog_ref[...]) * LOG2E

    row_i = lax.broadcasted_iota(jnp.int32, (chunk, chunk), 0)
    col_i = lax.broadcasted_iota(jnp.int32, (chunk, chunk), 1)
    causal = row_i >= col_i
    tril = causal.astype(BF16)
    low_half = col_i < SSM_HEAD_DIM

    for c in range(t_tile // chunk):
        rows = slice(c * chunk, (c + 1) * chunk)
        if c % PROJ_CHUNKS == 0:
            prow = slice(c * chunk, (c + PROJ_CHUNKS) * chunk)
            pieces = _proj_pieces(prow, yn_ref, ya_ref, ga_ref, gb_ref, h_ref, woa_ref,
                                  wos_ref, wmx_ref, o_ref, pyb_ref, pya_ref, mg_ref)
            n_early = h_ref.shape[1] // MXU_DIM
            units_left = PROJ_CHUNKS * n_groups * pairs

        dt_c = dt_ref[rows, :]
        acs = _sum3(_dot(tril, _split3(dt_c * a_log2)))
        acs_t = acs.T

        def b_of(g):
            return xs_ref[rows, b_off + g * n_state:b_off + (g + 1) * n_state]

        def group_front(g):
            cm = xs_ref[rows, c_off + g * n_state:c_off + (g + 1) * n_state]
            grp_ref[g % 2, :, 0:chunk] = _dot_nt(cm, b_of(g))
            grp_ref[g % 2, :, chunk:] = _dot(cm, state_ref[g].astype(BF16))

        def unit_front(g, q):
            slot = (g * pairs + q) % 2
            h0 = g * hpg + 2 * q
            ch = slice(g * gw + q * LANES, g * gw + (q + 1) * LANES)
            cb = grp_ref[g % 2, :, 0:chunk]
            a0 = _lane_bcast(acs, h0, chunk)
            a1 = _lane_bcast(acs, h0 + 1, chunk)
            m0 = cb * jnp.where(causal, jnp.exp2(a0 - acs_t[h0:h0 + 1, :]), 0.0)
            m1 = cb * jnp.where(causal, jnp.exp2(a1 - acs_t[h0 + 1:h0 + 2, :]), 0.0)
            a_pair = jnp.where(low_half, a0, a1)
            xp = xs_ref[rows, ch].astype(F32)
            xdt = xp * jnp.where(low_half, _lane_bcast(dt_c, h0, LANES),
                                 _lane_bcast(dt_c, h0 + 1, LANES))
            m01 = jnp.concatenate([m0.astype(BF16), m1.astype(BF16)], axis=1)
            x01 = jnp.concatenate([jnp.where(low_half, xdt, 0.0).astype(BF16),
                                   jnp.where(low_half, 0.0, xdt).astype(BF16)], axis=0)
            pair_ref[slot] = _dot(m01, x01)
            y_off = grp_ref[g % 2, :, chunk + q * LANES:chunk + (q + 1) * LANES]
            tail_ref[slot] = y_off * jnp.exp2(a_pair) + dch_ref[:, ch] * xp
            total = a_pair[chunk - 1:chunk, :]
            return (xdt * jnp.exp2(total - a_pair)).astype(BF16), jnp.exp2(total)

        def unit_back(g, q):
            slot = (g * pairs + q) % 2
            ch = slice(g * gw + q * LANES, g * gw + (q + 1) * LANES)
            y = pair_ref[slot] + tail_ref[slot]
            yg_ref[g % 2, :, q * LANES:(q + 1) * LANES] = y * sz_ref[rows, ch].astype(F32)

        def group_back(g):
            y_g = yg_ref[g % 2]
            ms = jnp.mean(y_g * y_g, axis=-1, keepdims=True)
            gch = slice(g * gw, (g + 1) * gw)
            yn_ref[rows, gch] = (y_g * lax.rsqrt(ms + EPS) * ng_ref[:, gch]).astype(BF16)

        group_front(0)
        backs = []
        for g in range(n_groups):
            if g + 1 < n_groups:
                group_front(g + 1)
            xstate, decay_end = [], []
            for q in range(pairs):
                if pieces and (n_early > 0 or len(pieces) * 3 >= units_left):
                    pieces.pop(0)()
                    n_early -= 1
                units_left -= 1
                xs_q, de_q = unit_front(g, q)
                xstate.append(xs_q)
                decay_end.append(de_q)
                for back in backs:
                    back()
                backs = [functools.partial(unit_back, g, q)]
                if q == pairs - 1:
                    backs.append(functools.partial(group_back, g))
            xstate = jnp.concatenate(xstate, axis=1)
            decay_end = jnp.concatenate(decay_end, axis=1)
            state_ref[g] = state_ref[g] * decay_end + _dot_tn(b_of(g), xstate)
        for back in backs:
            back()
        if (c + 1) % PROJ_CHUNKS == 0:
            while pieces:
                pieces.pop(0)()


def _mixer(h, proj, a_log_row, d_skip, norm_g, w_out_a, w_out_ssm, w_mix_out, seq):
    ya, sz, xs, ga, gb, dt = proj
    n_tok, d = h.shape
    d_inner = sz.shape[1]
    t = MIX_TOKENS
    n_tiles = n_tok // t
    ssd_row = lambda i: (jnp.minimum(i, n_tiles - 1), 0)
    proj_row = lambda i: (jnp.maximum(i - 1, 0), 0)
    vec_args = [a_log_row,
                jnp.repeat(d_skip.astype(F32), SSM_HEAD_DIM).reshape(1, d_inner),
                norm_g.reshape(1, d_inner).astype(F32)]
    w_args = [w_out_a.astype(BF16), w_out_ssm.astype(BF16), w_mix_out.astype(BF16)]
    proj_args = [ya, ga, gb, h]
    ssd_args = [sz, xs, dt]
    in_specs = ([pl.BlockSpec((t, a.shape[1]), proj_row) for a in proj_args]
                + [pl.BlockSpec((t, a.shape[1]), ssd_row) for a in ssd_args]
                + [_resident(a.shape) for a in vec_args + w_args])
    scratch = [
        pltpu.VMEM((SSM_GROUPS, SSM_STATE, d_inner // SSM_GROUPS), F32),
        pltpu.VMEM((t, d_inner), BF16),
        pltpu.VMEM((2, SSM_CHUNK, SSM_CHUNK + d_inner // SSM_GROUPS), F32),
        pltpu.VMEM((2, SSM_CHUNK, LANES), F32),
        pltpu.VMEM((2, SSM_CHUNK, LANES), F32),
        pltpu.VMEM((2, SSM_CHUNK, d_inner // SSM_GROUPS), F32),
        pltpu.VMEM((d // MXU_DIM, PROJ_CHUNKS * SSM_CHUNK, MXU_DIM), F32),
        pltpu.VMEM((d // MXU_DIM, PROJ_CHUNKS * SSM_CHUNK, MXU_DIM), F32),
        pltpu.VMEM((PROJ_CHUNKS * SSM_CHUNK, d), BF16),
    ]
    return pl.pallas_call(
        functools.partial(_mixer_body, d_inner=d_inner, n_groups=SSM_GROUPS,
                          n_state=SSM_STATE, tiles_per_seq=seq // t),
        grid=(n_tiles + 1,),
        in_specs=in_specs,
        out_specs=pl.BlockSpec((t, d), proj_row),
        out_shape=jax.ShapeDtypeStruct((n_tok, d), F32),
        scratch_shapes=scratch,
        compiler_params=_params("arbitrary"),
        name="mixer",
    )(*proj_args, *ssd_args, *vec_args, *w_args)


def _kv_body(m_ref, g_ref, w_ref, o_ref):
    u = _rms(m_ref[...], g_ref[...]).astype(BF16)
    o_ref[...] = _dot(u, w_ref[...]).astype(BF16)


def _kv_proj(mem2d, norm_g, w_kv):
    n_tok, d = mem2d.shape
    tm = KV_TOKENS
    return pl.pallas_call(
        _kv_body,
        grid=(n_tok // tm,),
        in_specs=[pl.BlockSpec((tm, d), lambda i: (i, 0)), _resident((1, d)),
                  _resident(w_kv.shape)],
        out_specs=pl.BlockSpec((tm, w_kv.shape[1]), lambda i: (i, 0)),
        out_shape=jax.ShapeDtypeStruct((n_tok, w_kv.shape[1]), BF16),
        compiler_params=_params("parallel"),
        name="kv_proj",
    )(mem2d, norm_g.reshape(1, d), w_kv.astype(BF16))


def _xattn_body(hp_ref, h_ref, g_ref, wq_ref, k_ref, v_ref, wo_ref, o_ref,
                att_ref, q_ref, s_ref, pv_ref, *, heads):
    d = h_ref.shape[1]
    dh = d // heads
    scale = 1.0 / math.sqrt(dh)
    ch = [slice(j * dh, (j + 1) * dh) for j in range(heads)]

    @pl.when(pl.program_id(0) == 0)
    def _():
        att_ref[...] = jnp.zeros_like(att_ref)

    att_prev = att_ref[...]
    u = _rms(h_ref[...], g_ref[...]).astype(BF16)

    def out_piece(j):
        o_ref[:, ch[j]] = hp_ref[:, ch[j]] + _dot(att_prev, wo_ref[:, ch[j]])

    def q_proj(j):
        q_ref[j % 2] = _dot(u, wq_ref[:, ch[j]])

    def scores(j):
        s_ref[j % 2] = _dot_nt(q_ref[j % 2].astype(BF16), k_ref[0, :, ch[j]])

    def attend(j):
        s = s_ref[j % 2] * scale
        p = jnp.exp(s - jnp.max(s, axis=-1, keepdims=True))
        p = p / jnp.sum(p, axis=-1, keepdims=True)
        pv_ref[j % 2] = _dot(p.astype(BF16), v_ref[0, :, ch[j]])

    def collect(j):
        att_ref[:, ch[j]] = pv_ref[j % 2].astype(BF16)

    out_piece(0)
    q_proj(0)
    scores(0)
    for j in range(heads):
        if j + 1 < heads:
            out_piece(j + 1)
            q_proj(j + 1)
            scores(j + 1)
        attend(j)
        if j > 0:
            collect(j - 1)
    collect(heads - 1)


def _xattn(h, kv, norm_g, w_q, w_o, batch, seq):
    n_tok, d = h.shape
    m = kv.shape[0] // batch
    kv3 = kv.reshape(batch, m, 2 * d)
    tm = ATTN_TOKENS
    tiles = seq // tm
    n_tiles = n_tok // tm
    dh = d // XATTN_HEADS
    cur = lambda i: jnp.minimum(i, n_tiles - 1)
    prev_row = lambda i: (jnp.maximum(i - 1, 0), 0)
    return pl.pallas_call(
        functools.partial(_xattn_body, heads=XATTN_HEADS),
        grid=(n_tiles + 1,),
        in_specs=[pl.BlockSpec((tm, d), prev_row),
                  pl.BlockSpec((tm, d), lambda i: (cur(i), 0)),
                  _resident((1, d)), _resident((d, d)),
                  pl.BlockSpec((1, m, d), lambda i: (cur(i) // tiles, 0, 0)),
                  pl.BlockSpec((1, m, d), lambda i: (cur(i) // tiles, 0, 1)),
                  _resident((d, d))],
        out_specs=pl.BlockSpec((tm, d), prev_row),
        out_shape=jax.ShapeDtypeStruct((n_tok, d), F32),
        scratch_shapes=[pltpu.VMEM((tm, d), BF16),
                        pltpu.VMEM((2, tm, dh), F32),
                        pltpu.VMEM((2, tm, m), F32),
                        pltpu.VMEM((2, tm, dh), F32)],
        compiler_params=_params("arbitrary"),
        name="xattn",
    )(h, h, norm_g.reshape(1, d), w_q.astype(BF16), kv3, kv3, w_o.astype(BF16))


def _split_w_in(w_in, d, d_inner, conv_ch, n_heads):
    n_front = 3 * d + d_inner + conv_ch
    w_front = w_in[:, :n_front].astype(BF16)
    w_gates = w_in[:, n_front + n_heads:].astype(BF16)
    w_dt = jnp.pad(w_in[:, n_front:n_front + n_heads], ((0, 0), (0, LANES - n_heads)))
    return w_front, w_gates, w_dt.astype(BF16)


def kernel(x, mem, ffn1_norm, ffn1_w_gate_up, ffn1_w_down, mix_norm, w_in, conv_a_w, w_out_a, ssm_conv_w, ssm_conv_b, ssm_dt_bias, ssm_a_log, ssm_d, ssm_norm, w_out_ssm, w_mix_out, xattn_norm, mem_norm, w_q, w_kv, w_o_x, ffn2_norm, ffn2_w_gate_up, ffn2_w_down, final_norm):
    batch, seq, d = x.shape
    depth = w_in.shape[0]
    d_inner = w_out_ssm.shape[1]
    conv_ch = ssm_conv_w.shape[2]
    n_heads = ssm_d.shape[1]
    h = x.reshape(batch * seq, d)
    mem2d = mem.reshape(-1, d)

    def head_row(v):
        return jnp.pad(v.astype(F32), (0, LANES - n_heads)).reshape(1, LANES)

    for i in range(depth):
        last = i == depth - 1
        h = _ffn(h, ffn1_norm[i], ffn1_w_gate_up[i], ffn1_w_down[i])
        w_front, w_gates, w_dt = _split_w_in(w_in[i], d, d_inner, conv_ch, n_heads)
        proj = _inproj(h, mix_norm[i], w_front, w_gates, w_dt, conv_a_w[i], ssm_conv_w[i],
                       ssm_conv_b[i], head_row(ssm_dt_bias[i]), d_inner, seq)
        h = _mixer(h, proj, head_row(ssm_a_log[i]), ssm_d[i], ssm_norm[i], w_out_a[i],
                   w_out_ssm[i], w_mix_out[i], seq)
        kv = _kv_proj(mem2d, mem_norm[i], w_kv[i])
        h = _xattn(h, kv, xattn_norm[i], w_q[i], w_o_x[i], batch, seq)
        h = _ffn(h, ffn2_norm[i], ffn2_w_gate_up[i], ffn2_w_down[i],
                 final_g=final_norm if last else None)
    return h.reshape(batch, seq, d)
```

```python
import functools
import math

import jax
import jax.numpy as jnp
from jax import lax
from jax.experimental import pallas as pl
from jax.experimental.pallas import tpu as pltpu

F32 = jnp.float32
BF16 = jnp.bfloat16

EPS = 1e-6
FFN_RES_WEIGHT = 0.5
SSM_HEAD_DIM = 64
SSM_GROUPS = 4
SSM_STATE = 128
SSM_CHUNK = 128
XATTN_HEADS = 4
LOG2E = math.log2(math.e)

LANES = 128
SUBLANES = 8
MXU_DIM = 256
VMEM_LIMIT_BYTES = 60 * 1024 * 1024

FFN_TOKENS = 1024
FFN_COLS = 1024
PROJ_TOKENS = 512
PROJ_COLS = 512
MIX_TOKENS = 512
ATTN_TOKENS = 1024
KV_TOKENS = 256


def _dot(a, b):
    return jnp.dot(a, b, preferred_element_type=F32)


def _dot_nt(a, b):
    return lax.dot_general(a, b, (((1,), (1,)), ((), ())), preferred_element_type=F32)


def _dot_tn(a, b):
    return lax.dot_general(a, b, (((0,), (0,)), ((), ())), preferred_element_type=F32)


def _rms(x, g):
    ms = jnp.mean(x * x, axis=-1, keepdims=True)
    return x * lax.rsqrt(ms + EPS) * g


def _silu(x):
    return x * jax.nn.sigmoid(x)


def _resident(shape):
    zeros = (0,) * len(shape)
    return pl.BlockSpec(shape, lambda *_: zeros, pipeline_mode=pl.Buffered(1))


def _params(*semantics, flags=None):
    return pltpu.CompilerParams(dimension_semantics=semantics,
                                vmem_limit_bytes=VMEM_LIMIT_BYTES, flags=flags)


def _ff_chunks(d_ff):
    chunks, start = [], 0
    while start < d_ff:
        n = min(FFN_COLS, d_ff - start)
        chunks.append((start, n))
        start += n
    return tuple(chunks)


def _ffn_body(x_ref, g_ref, wgu_ref, wd_ref, *rest, chunks, d_ff, final):
    if final:
        fg_ref, o_ref, gu_ref = rest
    else:
        o_ref, gu_ref = rest
    x = x_ref[...]
    u = _rms(x, g_ref[...]).astype(BF16)
    cw = gu_ref.shape[2] // 2

    def front(j):
        start, n = chunks[j]
        gu_ref[j % 2, :, 0:n] = _dot(u, wgu_ref[:, start:start + n])
        gu_ref[j % 2, :, cw:cw + n] = _dot(u, wgu_ref[:, d_ff + start:d_ff + start + n])

    def back(j):
        start, n = chunks[j]
        act = _silu(gu_ref[j % 2, :, 0:n]) * gu_ref[j % 2, :, cw:cw + n]
        return _dot(act.astype(BF16), wd_ref[start:start + n, :])

    acc = None
    front(0)
    for j in range(len(chunks)):
        if j + 1 < len(chunks):
            front(j + 1)
        part = back(j)
        acc = part if acc is None else acc + part
    h = x + FFN_RES_WEIGHT * acc
    if final:
        h = _rms(h, fg_ref[...])
    o_ref[...] = h


def _ffn(x, norm_g, w_gate_up, w_down, final_g=None):
    n_tok, d = x.shape
    d_ff = w_down.shape[0]
    tm = FFN_TOKENS
    tile = pl.BlockSpec((tm, d), lambda i: (i, 0))
    in_specs = [tile, _resident((1, d)), _resident((d, 2 * d_ff)), _resident((d_ff, d))]
    args = [x, norm_g.reshape(1, d), w_gate_up.astype(BF16), w_down.astype(BF16)]
    if final_g is not None:
        in_specs.append(_resident((1, d)))
        args.append(final_g.reshape(1, d))
    return pl.pallas_call(
        functools.partial(_ffn_body, chunks=_ff_chunks(d_ff), d_ff=d_ff,
                          final=final_g is not None),
        grid=(n_tok // tm,),
        in_specs=in_specs,
        out_specs=tile,
        out_shape=jax.ShapeDtypeStruct((n_tok, d), F32),
        scratch_shapes=[pltpu.VMEM((2, tm, 2 * FFN_COLS), F32)],
        compiler_params=_params("parallel"),
        name="ffn_final" if final_g is not None else "ffn",
    )(*args)


PROJ_CHUNKS = 2
RING_SLOTS = 3


def _conv_rows(ring_ref, slot, tm, taps):
    k_taps = taps.shape[0]
    acc = None
    for back in range(k_taps):
        shifted = ring_ref[slot, SUBLANES - back:SUBLANES - back + tm, :]
        term = shifted * taps[k_taps - 1 - back:k_taps - back, :]
        acc = term if acc is None else acc + term
    return acc


def _inproj_body(h_ref, g_ref, wf_ref, wg_ref, wdt_ref, caw_ref, csw_ref, csb_ref, dtb_ref,
                 ya_ref, sz_ref, xs_ref, ga_ref, gb_ref, dt_ref,
                 ring_ref, side_ref, halo_c_ref, halo_x_ref, *, tiles_per_seq):
    tm, d = h_ref.shape
    d_inner = sz_ref.shape[1]
    conv_ch = xs_ref.shape[1]
    cw = PROJ_COLS

    @pl.when(pl.program_id(0) % tiles_per_seq == 0)
    def _():
        halo_c_ref[...] = jnp.zeros_like(halo_c_ref)
        halo_x_ref[...] = jnp.zeros_like(halo_x_ref)

    u = _rms(h_ref[...], g_ref[...]).astype(BF16)
    n_front = wf_ref.shape[1]

    def proj(off):
        if off < n_front:
            return _dot(u, wf_ref[:, off:off + cw])
        return _dot(u, wg_ref[:, off - n_front:off - n_front + cw])

    def park(slot, x, halo_ref=None, cols=None):
        if halo_ref is not None:
            ring_ref[slot, 0:SUBLANES, :] = halo_ref[:, cols]
            halo_ref[:, cols] = x[tm - SUBLANES:, :]
        ring_ref[slot, SUBLANES:SUBLANES + tm, :] = x

    z_off, x_off, g_off = 3 * d, 3 * d + d_inner, 3 * d + d_inner + conv_ch

    def conv3_job(s):
        cols = slice(s, s + cw)

        def front(slot):
            park(slot, proj(d + s) * proj(2 * d + s), halo_c_ref, cols)
            side_ref[slot] = proj(s)

        def back(slot):
            conv = _conv_rows(ring_ref, slot, tm, caw_ref[:, cols])
            ya_ref[:, cols] = (side_ref[slot] * conv).astype(BF16)
        return front, back

    def conv4_job(s):
        cols = slice(s, s + cw)

        def front(slot):
            park(slot, proj(x_off + s), halo_x_ref, cols)

        def back(slot):
            conv = _conv_rows(ring_ref, slot, tm, csw_ref[:, cols])
            xs_ref[:, cols] = _silu(conv + csb_ref[:, cols]).astype(BF16)
        return front, back

    def gate_job(o_ref, off, s, fn):
        def front(slot):
            park(slot, proj(off + s))

        def back(slot):
            o_ref[:, s:s + cw] = fn(ring_ref[slot, SUBLANES:SUBLANES + tm, :]).astype(BF16)
        return front, back

    heavy = [conv4_job(s) for s in range(0, conv_ch, cw)]
    light = ([gate_job(sz_ref, z_off, s, _silu) for s in range(0, d_inner, cw)]
             + [conv3_job(s) for s in range(0, d, cw)]
             + [gate_job(ga_ref, g_off, s, jax.nn.sigmoid) for s in range(0, d, cw)]
             + [gate_job(gb_ref, g_off + d, s, jax.nn.sigmoid) for s in range(0, d, cw)])
    jobs = []
    while heavy or light:
        if heavy:
            jobs.append(heavy.pop(0))
        if light:
            jobs.append(light.pop(0))
    pending = None
    for j, (front, back) in enumerate(jobs):
        front(j % RING_SLOTS)
        if pending is not None:
            pending()
        pending = functools.partial(back, j % RING_SLOTS)
    dt_ref[...] = jax.nn.softplus(_dot(u, wdt_ref[...]) + dtb_ref[...])
    pending()


def _inproj(h, norm_g, w_front, w_gates, w_dt, conv_a_w, ssm_conv_w, ssm_conv_b, dt_bias_row,
            d_inner, seq):
    n_tok, d = h.shape
    conv_ch = ssm_conv_w.shape[1]
    tm = PROJ_TOKENS
    row = lambda i: (i, 0)
    widths = (d, d_inner, conv_ch, d, d)
    out_shape = [jax.ShapeDtypeStruct((n_tok, n), BF16) for n in widths]
    out_shape.append(jax.ShapeDtypeStruct((n_tok, LANES), F32))
    out_specs = [pl.BlockSpec((tm, n), row) for n in widths]
    out_specs.append(pl.BlockSpec((tm, LANES), row))
    vec_args = [norm_g.reshape(1, d), w_front, w_gates, w_dt, conv_a_w.astype(F32),
                ssm_conv_w.astype(F32), ssm_conv_b.reshape(1, conv_ch).astype(F32),
                dt_bias_row]
    return pl.pallas_call(
        functools.partial(_inproj_body, tiles_per_seq=seq // tm),
        grid=(n_tok // tm,),
        in_specs=[pl.BlockSpec((tm, d), row)] + [_resident(a.shape) for a in vec_args],
        out_specs=out_specs,
        out_shape=out_shape,
        scratch_shapes=[pltpu.VMEM((RING_SLOTS, tm + SUBLANES, PROJ_COLS), F32),
                        pltpu.VMEM((RING_SLOTS, tm, PROJ_COLS), F32),
                        pltpu.VMEM((SUBLANES, d), F32),
                        pltpu.VMEM((SUBLANES, conv_ch), F32)],
        compiler_params=_params("arbitrary"),
        name="inproj",
    )(h, *vec_args)


def _split3(v):
    hi = v.astype(BF16)
    r1 = v - hi.astype(F32)
    mid = r1.astype(BF16)
    lo = (r1 - mid.astype(F32)).astype(BF16)
    return jnp.concatenate([hi, mid, lo], axis=1)


def _sum3(v):
    n = v.shape[1] // 3
    return v[:, :n] + v[:, n:2 * n] + v[:, 2 * n:]


def _lane_bcast(m, col, width):
    return jnp.broadcast_to(m[:, col:col + 1], (m.shape[0], width))


def _proj_pieces(rows, yn_ref, ya_ref, ga_ref, gb_ref, h_ref, woa_ref, wos_ref, wmx_ref,
                 o_ref, pyb_ref, pya_ref, mg_ref):
    d = h_ref.shape[1]
    n_tiles = d // MXU_DIM
    cols = [slice(n * MXU_DIM, (n + 1) * MXU_DIM) for n in range(n_tiles)]

    def merge(n):
        mg_ref[:, cols[n]] = (ga_ref[rows, cols[n]].astype(F32) * pya_ref[n]
                              + gb_ref[rows, cols[n]].astype(F32) * pyb_ref[n]).astype(BF16)

    def yb_piece(n):
        def run():
            pyb_ref[n] = _dot(yn_ref[rows, :], wos_ref[:, cols[n]])
        return run

    def ya_piece(n):
        def run():
            pya_ref[n] = _dot(ya_ref[rows, :], woa_ref[:, cols[n]])
            if n > 0:
                merge(n - 1)
        return run

    def mix_piece(n):
        def run():
            if n == 0:
                merge(n_tiles - 1)
            o_ref[rows, cols[n]] = h_ref[rows, cols[n]] + _dot(mg_ref[...], wmx_ref[:, cols[n]])
        return run

    return ([yb_piece(n) for n in range(n_tiles)] + [ya_piece(n) for n in range(n_tiles)]
            + [mix_piece(n) for n in range(n_tiles)])


def _mixer_body(ya_ref, ga_ref, gb_ref, h_ref, sz_ref, xs_ref, dt_ref,
                alog_ref, dch_ref, ng_ref, woa_ref, wos_ref, wmx_ref, o_ref,
                state_ref, yn_ref, grp_ref, pair_ref, tail_ref, yg_ref,
                pyb_ref, pya_ref, mg_ref, *, d_inner, n_groups, n_state, tiles_per_seq):
    t_tile = h_ref.shape[0]
    chunk = SSM_CHUNK
    gw = d_inner // n_groups
    pairs = gw // LANES
    hpg = gw // SSM_HEAD_DIM
    b_off, c_off = d_inner, d_inner + n_groups * n_state
    assert h_ref.shape[1] // MXU_DIM <= pairs
    step = pl.program_id(0)

    @pl.when(step == 0)
    def _():
        yn_ref[...] = jnp.zeros_like(yn_ref)

    @pl.when(step % tiles_per_seq == 0)
    def _():
        state_ref[...] = jnp.zeros_like(state_ref)

    a_log2 = -jnp.exp(alog_ref[...]) * LOG2E

    row_i = lax.broadcasted_iota(jnp.int32, (chunk, chunk), 0)
    col_i = lax.broadcasted_iota(jnp.int32, (chunk, chunk), 1)
    causal = row_i >= col_i
    tril = causal.astype(BF16)
    low_half = col_i < SSM_HEAD_DIM

    for c in range(t_tile // chunk):
        rows = slice(c * chunk, (c + 1) * chunk)
        if c % PROJ_CHUNKS == 0:
            prow = slice(c * chunk, (c + PROJ_CHUNKS) * chunk)
            pieces = _proj_pieces(prow, yn_ref, ya_ref, ga_ref, gb_ref, h_ref, woa_ref,
                                  wos_ref, wmx_ref, o_ref, pyb_ref, pya_ref, mg_ref)
            n_early = h_ref.shape[1] // MXU_DIM
            units_left = PROJ_CHUNKS * n_groups * pairs

        dt_c = dt_ref[rows, :]
        acs = _sum3(_dot(tril, _split3(dt_c * a_log2)))
        acs_t = acs.T

        def b_of(g):
            return xs_ref[rows, b_off + g * n_state:b_off + (g + 1) * n_state]

        def group_front(g):
            cm = xs_ref[rows, c_off + g * n_state:c_off + (g + 1) * n_state]
            grp_ref[g % 2, :, 0:chunk] = _dot_nt(cm, b_of(g))
            grp_ref[g % 2, :, chunk:] = _dot(cm, state_ref[g].astype(BF16))

        def unit_front(g, q):
            slot = (g * pairs + q) % 2
            h0 = g * hpg + 2 * q
            ch = slice(g * gw + q * LANES, g * gw + (q + 1) * LANES)
            cb = grp_ref[g % 2, :, 0:chunk]
            a0 = _lane_bcast(acs, h0, chunk)
            a1 = _lane_bcast(acs, h0 + 1, chunk)
            m0 = cb * jnp.where(causal, jnp.exp2(a0 - acs_t[h0:h0 + 1, :]), 0.0)
            m1 = cb * jnp.where(causal, jnp.exp2(a1 - acs_t[h0 + 1:h0 + 2, :]), 0.0)
            a_pair = jnp.where(low_half, a0, a1)
            xp = xs_ref[rows, ch].astype(F32)
            xdt = xp * jnp.where(low_half, _lane_bcast(dt_c, h0, LANES),
                                 _lane_bcast(dt_c, h0 + 1, LANES))
            m01 = jnp.concatenate([m0.astype(BF16), m1.astype(BF16)], axis=1)
            x01 = jnp.concatenate([jnp.where(low_half, xdt, 0.0).astype(BF16),
                                   jnp.where(low_half, 0.0, xdt).astype(BF16)], axis=0)
            pair_ref[slot] = _dot(m01, x01)
            y_off = grp_ref[g % 2, :, chunk + q * LANES:chunk + (q + 1) * LANES]
            tail_ref[slot] = y_off * jnp.exp2(a_pair) + dch_ref[:, ch] * xp
            total = a_pair[chunk - 1:chunk, :]
            return (xdt * jnp.exp2(total - a_pair)).astype(BF16), jnp.exp2(total)

        def unit_back(g, q):
            slot = (g * pairs + q) % 2
            ch = slice(g * gw + q * LANES, g * gw + (q + 1) * LANES)
            y = pair_ref[slot] + tail_ref[slot]
            yg_ref[g % 2, :, q * LANES:(q + 1) * LANES] = y * sz_ref[rows, ch].astype(F32)

        def group_back(g):
            y_g = yg_ref[g % 2]
            ms = jnp.mean(y_g * y_g, axis=-1, keepdims=True)
            gch = slice(g * gw, (g + 1) * gw)
            yn_ref[rows, gch] = (y_g * lax.rsqrt(ms + EPS) * ng_ref[:, gch]).astype(BF16)

        group_front(0)
        backs = []
        for g in range(n_groups):
            if g + 1 < n_groups:
                group_front(g + 1)
            xstate, decay_end = [], []
            for q in range(pairs):
                if pieces and (n_early > 0 or len(pieces) * 3 >= units_left):
                    pieces.pop(0)()
                    n_early -= 1
                units_left -= 1
                xs_q, de_q = unit_front(g, q)
                xstate.append(xs_q)
                decay_end.append(de_q)
                for back in backs:
                    back()
                backs = [functools.partial(unit_back, g, q)]
                if q == pairs - 1:
                    backs.append(functools.partial(group_back, g))
            xstate = jnp.concatenate(xstate, axis=1)
            decay_end = jnp.concatenate(decay_end, axis=1)
            state_ref[g] = state_ref[g] * decay_end + _dot_tn(b_of(g), xstate)
        for back in backs:
            back()
        if (c + 1) % PROJ_CHUNKS == 0:
            while pieces:
                pieces.pop(0)()


def _mixer(h, proj, a_log_row, d_skip, norm_g, w_out_a, w_out_ssm, w_mix_out, seq):
    ya, sz, xs, ga, gb, dt = proj
    n_tok, d = h.shape
    d_inner = sz.shape[1]
    t = MIX_TOKENS
    n_tiles = n_tok // t
    ssd_row = lambda i: (jnp.minimum(i, n_tiles - 1), 0)
    proj_row = lambda i: (jnp.maximum(i - 1, 0), 0)
    vec_args = [a_log_row,
                jnp.repeat(d_skip.astype(F32), SSM_HEAD_DIM).reshape(1, d_inner),
                norm_g.reshape(1, d_inner).astype(F32)]
    w_args = [w_out_a.astype(BF16), w_out_ssm.astype(BF16), w_mix_out.astype(BF16)]
    proj_args = [ya, ga, gb, h]
    ssd_args = [sz, xs, dt]
    in_specs = ([pl.BlockSpec((t, a.shape[1]), proj_row) for a in proj_args]
                + [pl.BlockSpec((t, a.shape[1]), ssd_row) for a in ssd_args]
                + [_resident(a.shape) for a in vec_args + w_args])
    scratch = [
        pltpu.VMEM((SSM_GROUPS, SSM_STATE, d_inner // SSM_GROUPS), F32),
        pltpu.VMEM((t, d_inner), BF16),
        pltpu.VMEM((2, SSM_CHUNK, SSM_CHUNK + d_inner // SSM_GROUPS), F32),
        pltpu.VMEM((2, SSM_CHUNK, LANES), F32),
        pltpu.VMEM((2, SSM_CHUNK, LANES), F32),
        pltpu.VMEM((2, SSM_CHUNK, d_inner // SSM_GROUPS), F32),
        pltpu.VMEM((d // MXU_DIM, PROJ_CHUNKS * SSM_CHUNK, MXU_DIM), F32),
        pltpu.VMEM((d // MXU_DIM, PROJ_CHUNKS * SSM_CHUNK, MXU_DIM), F32),
        pltpu.VMEM((PROJ_CHUNKS * SSM_CHUNK, d), BF16),
    ]
    return pl.pallas_call(
        functools.partial(_mixer_body, d_inner=d_inner, n_groups=SSM_GROUPS,
                          n_state=SSM_STATE, tiles_per_seq=seq // t),
        grid=(n_tiles + 1,),
        in_specs=in_specs,
        out_specs=pl.BlockSpec((t, d), proj_row),
        out_shape=jax.ShapeDtypeStruct((n_tok, d), F32),
        scratch_shapes=scratch,
        compiler_params=_params("arbitrary"),
        name="mixer",
    )(*proj_args, *ssd_args, *vec_args, *w_args)


def _kv_body(m_ref, g_ref, w_ref, o_ref):
    u = _rms(m_ref[...], g_ref[...]).astype(BF16)
    o_ref[...] = _dot(u, w_ref[...]).astype(BF16)


def _kv_proj(mem2d, norm_g, w_kv):
    n_tok, d = mem2d.shape
    tm = KV_TOKENS
    return pl.pallas_call(
        _kv_body,
        grid=(n_tok // tm,),
        in_specs=[pl.BlockSpec((tm, d), lambda i: (i, 0)), _resident((1, d)),
                  _resident(w_kv.shape)],
        out_specs=pl.BlockSpec((tm, w_kv.shape[1]), lambda i: (i, 0)),
        out_shape=jax.ShapeDtypeStruct((n_tok, w_kv.shape[1]), BF16),
        compiler_params=_params("parallel"),
        name="kv_proj",
    )(mem2d, norm_g.reshape(1, d), w_kv.astype(BF16))


def _xattn_body(hp_ref, h_ref, g_ref, wq_ref, k_ref, v_ref, wo_ref, o_ref,
                att_ref, q_ref, s_ref, pv_ref, *, heads):
    d = h_ref.shape[1]
    dh = d // heads
    scale = 1.0 / math.sqrt(dh)
    ch = [slice(j * dh, (j + 1) * dh) for j in range(heads)]

    @pl.when(pl.program_id(0) == 0)
    def _():
        att_ref[...] = jnp.zeros_like(att_ref)

    att_prev = att_ref[...]
    u = _rms(h_ref[...], g_ref[...]).astype(BF16)

    def out_piece(j):
        o_ref[:, ch[j]] = hp_ref[:, ch[j]] + _dot(att_prev, wo_ref[:, ch[j]])

    def q_proj(j):
        q_ref[j % 2] = _dot(u, wq_ref[:, ch[j]])

    def scores(j):
        s_ref[j % 2] = _dot_nt(q_ref[j % 2].astype(BF16), k_ref[0, :, ch[j]])

    def attend(j):
        s = s_ref[j % 2] * scale
        p = jnp.exp(s - jnp.max(s, axis=-1, keepdims=True))
        p = p / jnp.sum(p, axis=-1, keepdims=True)
        pv_ref[j % 2] = _dot(p.astype(BF16), v_ref[0, :, ch[j]])

    def collect(j):
        att_ref[:, ch[j]] = pv_ref[j % 2].astype(BF16)

    out_piece(0)
    q_proj(0)
    scores(0)
    for j in range(heads):
        if j + 1 < heads:
            out_piece(j + 1)
            q_proj(j + 1)
            scores(j + 1)
        attend(j)
        if j > 0:
            collect(j - 1)
    collect(heads - 1)


def _xattn(h, kv, norm_g, w_q, w_o, batch, seq):
    n_tok, d = h.shape
    m = kv.shape[0] // batch
    kv3 = kv.reshape(batch, m, 2 * d)
    tm = ATTN_TOKENS
    tiles = seq // tm
    n_tiles = n_tok // tm
    dh = d // XATTN_HEADS
    cur = lambda i: jnp.minimum(i, n_tiles - 1)
    prev_row = lambda i: (jnp.maximum(i - 1, 0), 0)
    return pl.pallas_call(
        functools.partial(_xattn_body, heads=XATTN_HEADS),
        grid=(n_tiles + 1,),
        in_specs=[pl.BlockSpec((tm, d), prev_row),
                  pl.BlockSpec((tm, d), lambda i: (cur(i), 0)),
                  _resident((1, d)), _resident((d, d)),
                  pl.BlockSpec((1, m, d), lambda i: (cur(i) // tiles, 0, 0)),
                  pl.BlockSpec((1, m, d), lambda i: (cur(i) // tiles, 0, 1)),
                  _resident((d, d))],
        out_specs=pl.BlockSpec((tm, d), prev_row),
        out_shape=jax.ShapeDtypeStruct((n_tok, d), F32),
        scratch_shapes=[pltpu.VMEM((tm, d), BF16),
                        pltpu.VMEM((2, tm, dh), F32),
                        pltpu.VMEM((2, tm, m), F32),
                        pltpu.VMEM((2, tm, dh), F32)],
        compiler_params=_params("arbitrary"),
        name="xattn",
    )(h, h, norm_g.reshape(1, d), w_q.astype(BF16), kv3, kv3, w_o.astype(BF16))


def _split_w_in(w_in, d, d_inner, conv_ch, n_heads):
    n_front = 3 * d + d_inner + conv_ch
    w_front = w_in[:, :n_front].astype(BF16)
    w_gates = w_in[:, n_front + n_heads:].astype(BF16)
    w_dt = jnp.pad(w_in[:, n_front:n_front + n_heads], ((0, 0), (0, LANES - n_heads)))
    return w_front, w_gates, w_dt.astype(BF16)


def kernel(x, mem, ffn1_norm, ffn1_w_gate_up, ffn1_w_down, mix_norm, w_in, conv_a_w, w_out_a, ssm_conv_w, ssm_conv_b, ssm_dt_bias, ssm_a_log, ssm_d, ssm_norm, w_out_ssm, w_mix_out, xattn_norm, mem_norm, w_q, w_kv, w_o_x, ffn2_norm, ffn2_w_gate_up, ffn2_w_down, final_norm):
    batch, seq, d = x.shape
    depth = w_in.shape[0]
    d_inner = w_out_ssm.shape[1]
    conv_ch = ssm_conv_w.shape[2]
    n_heads = ssm_d.shape[1]
    h = x.reshape(batch * seq, d)
    mem2d = mem.reshape(-1, d)

    def head_row(v):
        return jnp.pad(v.astype(F32), (0, LANES - n_heads)).reshape(1, LANES)

    for i in range(depth):
        last = i == depth - 1
        h = _ffn(h, ffn1_norm[i], ffn1_w_gate_up[i], ffn1_w_down[i])
        w_front, w_gates, w_dt = _split_w_in(w_in[i], d, d_inner, conv_ch, n_heads)
        proj = _inproj(h, mix_norm[i], w_front, w_gates, w_dt, conv_a_w[i], ssm_conv_w[i],
                       ssm_conv_b[i], head_row(ssm_dt_bias[i]), d_inner, seq)
        h = _mixer(h, proj, head_row(ssm_a_log[i]), ssm_d[i], ssm_norm[i], w_out_a[i],
                   w_out_ssm[i], w_mix_out[i], seq)
        kv = _kv_proj(mem2d, mem_norm[i], w_kv[i])
        h = _xattn(h, kv, xattn_norm[i], w_q[i], w_o_x[i], batch, seq)
        h = _ffn(h, ffn2_norm[i], ffn2_w_gate_up[i], ffn2_w_down[i],
                 final_g=final_norm if last else None)
    return h.reshape(batch, seq, d)
```
